```python
import math
import jax, jax.numpy as jnp
from jax import lax
import numpy as np

D_MODEL = 1024
BATCH = 2
SEQ = 16384
DEPTH = 2

D_MIX = D_MODEL
D_A = D_MIX // 4
H_A = 4
HD_A = D_A // H_A
CHUNK = 128
D_B = D_MIX // 2
H_B = 4
DH_B = D_B // (2 * H_B)
D_C = D_MIX // 4
G_C = 4
GD_C = D_C // G_C
POOL_WINDOWS = (2, 4, 8, 16)
D_IN = 2 * D_A + 3 * D_B + D_C
Q_BLOCK = 128
NUM_BUCKETS = 32
MAX_EXACT = NUM_BUCKETS // 2
MAX_DISTANCE = 128
D_FF = 2816
N_EXPERTS = 8
TOP_K = 2
D_FF_EXPERT = 3584
N_DENSE = (DEPTH + 1) // 2
N_MOE = DEPTH // 2
ALPHA = (2 * DEPTH) ** 0.25
BETA = (8 * DEPTH) ** -0.25
EPS = 1e-5

kernel_name = "hybrid_gmlp_diffattn_pool_moe_deepnorm"

F32 = jnp.float32


def layer_norm(x, g, b):
    xf = x.astype(F32)
    mu = jnp.mean(xf, axis=-1, keepdims=True)
    var = jnp.mean(jnp.square(xf - mu), axis=-1, keepdims=True)
    return ((xf - mu) * lax.rsqrt(var + EPS) * g.astype(F32) + b.astype(F32)).astype(x.dtype)


def rms_norm(x, g):
    xf = x.astype(F32)
    ms = jnp.mean(jnp.square(xf), axis=-1, keepdims=True)
    return (xf * lax.rsqrt(ms + EPS) * g.astype(F32)).astype(x.dtype)


def t5_bucket(dist):
    n = jnp.maximum(dist, 0)
    nf = jnp.maximum(n, 1).astype(F32)
    large = MAX_EXACT + (jnp.log(nf / MAX_EXACT) / math.log(MAX_DISTANCE / MAX_EXACT)
                         * (NUM_BUCKETS - MAX_EXACT)).astype(jnp.int32)
    large = jnp.minimum(large, NUM_BUCKETS - 1)
    return jnp.where(n < MAX_EXACT, n, large)


def gmlp_mixer(z, ws, bs):
    bn, s, _ = z.shape
    z = jax.nn.gelu(z, approximate=False)
    u, v = jnp.split(z, 2, axis=-1)
    v = v.reshape(bn, s // CHUNK, CHUNK, H_A, HD_A)
    vf = v.astype(F32)
    mu = jnp.mean(vf, axis=-1, keepdims=True)
    var = jnp.mean(jnp.square(vf - mu), axis=-1, keepdims=True)
    vn = ((vf - mu) * lax.rsqrt(var + EPS)).astype(z.dtype)
    mask = jnp.tril(jnp.ones((CHUNK, CHUNK), dtype=bool))
    w = jnp.where(mask[None], ws, jnp.zeros_like(ws))
    sg = jnp.einsum('hts,bcshd->bcthd', w, vn) + bs.T[None, None, :, :, None]
    return u * sg.reshape(bn, s, D_A)


def diff_attention(q, k, v, rel_bias, lam, subln_g, lam_init):
    bn, s, _ = q.shape
    q = q.reshape(bn, s, H_B, 2, DH_B).transpose(0, 2, 3, 1, 4)
    k = k.reshape(bn, s, H_B, 2, DH_B).transpose(0, 2, 3, 1, 4)
    v = v.reshape(bn, s, H_B, 2 * DH_B).transpose(0, 2, 1, 3)
    nb = s // Q_BLOCK
    qb = q.reshape(bn, H_B, 2, nb, Q_BLOCK, DH_B).transpose(3, 0, 1, 2, 4, 5)
    kpos = jnp.arange(s)
    scale = DH_B ** -0.5

    def block(args):
        i, qblk = args
        qpos = i * Q_BLOCK + jnp.arange(Q_BLOCK)
        dist = qpos[:, None] - kpos[None, :]
        bias = jnp.transpose(rel_bias[t5_bucket(dist)].astype(F32), (2, 0, 1))
        logits = jnp.einsum('bhmqd,bhmkd->bhmqk', qblk, k).astype(F32) * scale
        logits = logits + bias[None, :, None]
        logits = jnp.where(dist >= 0, logits, -jnp.inf)
        p = jax.nn.softmax(logits, axis=-1)
        a = p[:, :, 0] - lam * p[:, :, 1]
        return jnp.einsum('bhqk,bhkv->bhqv', a.astype(v.dtype), v)

    o = lax.map(block, (jnp.arange(nb), qb))
    o = o.transpose(1, 2, 0, 3, 4).reshape(bn, H_B, s, 2 * DH_B)
    o = rms_norm(o, subln_g) * (1.0 - lam_init)
    return o.transpose(0, 2, 1, 3).reshape(bn, s, D_B)


def pool_mixer(c, w_pool, pool_scale):
    bn, s, _ = c.shape
    cf = c.astype(F32).reshape(bn, s, G_C, GD_C)
    cs = jnp.concatenate([jnp.zeros((bn, 1, G_C, GD_C), F32), jnp.cumsum(cf, axis=1)], axis=1)
    t = jnp.arange(s)[:, None]
    win = jnp.array(POOL_WINDOWS, dtype=jnp.int32)[None, :]
    start = jnp.maximum(t + 1 - win, 0)
    count = (t + 1 - start).astype(F32)
    gidx = jnp.arange(G_C)[None, :]
    window_sum = cs[:, 1:] - cs[:, start, gidx]
    y = (window_sum / count[None, :, :, None] - cf).astype(c.dtype)
    y = jnp.einsum('bsgc,gcd->bsgd', y, w_pool).reshape(bn, s, D_C)
    return y * pool_scale


def swiglu(x, w1, w3, w2):
    return (jax.nn.silu(x @ w1) * (x @ w3)) @ w2


def moe_ffn(x, router_w, w1, w3, w2):
    bn, s, d = x.shape
    xt = x.reshape(bn * s, d)
    logits = (xt @ router_w).astype(F32)
    top_v, top_i = lax.top_k(logits, TOP_K)
    top_w = jax.nn.softmax(top_v, axis=-1)
    gates = jnp.sum(jax.nn.one_hot(top_i, N_EXPERTS, dtype=F32) * top_w[..., None], axis=1)
    y = jnp.zeros_like(xt)
    for e in range(N_EXPERTS):
        y = y + gates[:, e:e + 1].astype(x.dtype) * swiglu(xt, w1[e], w3[e], w2[e])
    return y.reshape(bn, s, d)


def setup_inputs(seed: int = 0) -> dict:
    key = jax.random.key(seed)
    ks = jax.random.split(key, 24)
    nrm = lambda k, shape, sc: jax.random.normal(k, shape, F32) * sc
    return {
        "x": nrm(ks[0], (BATCH, SEQ, D_MODEL), 1.0),
        "w_in": nrm(ks[1], (DEPTH, D_MODEL, D_IN), D_MODEL ** -0.5),
        "w_out": nrm(ks[2], (DEPTH, D_MIX, D_MODEL), BETA * D_MIX ** -0.5),
        "gmlp_ws": nrm(ks[3], (DEPTH, H_A, CHUNK, CHUNK), CHUNK ** -0.5),
        "gmlp_bs": 1.0 + nrm(ks[4], (DEPTH, H_A, CHUNK), 0.02),
        "lam_q1": nrm(ks[5], (DEPTH, DH_B), 0.1),
        "lam_k1": nrm(ks[6], (DEPTH, DH_B), 0.1),
        "lam_q2": nrm(ks[7], (DEPTH, DH_B), 0.1),
        "lam_k2": nrm(ks[8], (DEPTH, DH_B), 0.1),
        "diff_subln_g": 1.0 + nrm(ks[9], (DEPTH, 2 * DH_B), 0.02),
        "rel_bias": nrm(ks[10], (NUM_BUCKETS, H_B), 0.5),
        "pool_w": nrm(ks[11], (DEPTH, G_C, GD_C, GD_C), GD_C ** -0.5),
        "pool_scale": 1.0 + nrm(ks[12], (DEPTH, D_C), 0.1),
        "ln1_g": 1.0 + nrm(ks[13], (DEPTH, D_MODEL), 0.02),
        "ln1_b": nrm(ks[14], (DEPTH, D_MODEL), 0.02),
        "ln2_g": 1.0 + nrm(ks[15], (DEPTH, D_MODEL), 0.02),
        "ln2_b": nrm(ks[16], (DEPTH, D_MODEL), 0.02),
        "ffn_w1": nrm(ks[17], (N_DENSE, D_MODEL, D_FF), D_MODEL ** -0.5),
        "ffn_w3": nrm(ks[18], (N_DENSE, D_MODEL, D_FF), D_MODEL ** -0.5),
        "ffn_w2": nrm(ks[19], (N_DENSE, D_FF, D_MODEL), BETA * D_FF ** -0.5),
        "router_w": nrm(ks[20], (N_MOE, D_MODEL, N_EXPERTS), D_MODEL ** -0.5),
        "moe_w1": nrm(ks[21], (N_MOE, N_EXPERTS, D_MODEL, D_FF_EXPERT), D_MODEL ** -0.5),
        "moe_w3": nrm(ks[22], (N_MOE, N_EXPERTS, D_MODEL, D_FF_EXPERT), D_MODEL ** -0.5),
        "moe_w2": nrm(ks[23], (N_MOE, N_EXPERTS, D_FF_EXPERT, D_MODEL), BETA * D_FF_EXPERT ** -0.5),
    }


def reference(x, w_in, w_out, gmlp_ws, gmlp_bs, lam_q1, lam_k1, lam_q2, lam_k2,
              diff_subln_g, rel_bias, pool_w, pool_scale, ln1_g, ln1_b, ln2_g, ln2_b,
              ffn_w1, ffn_w3, ffn_w2, router_w, moe_w1, moe_w3, moe_w2):
    o_a = 2 * D_A
    o_q = o_a + D_B
    o_k = o_q + D_B
    o_v = o_k + D_B
    for l in range(DEPTH):
        p = x @ w_in[l]
        lam_init = 0.8 - 0.6 * math.exp(-0.3 * l)
        lam = (jnp.exp(jnp.sum(lam_q1[l].astype(F32) * lam_k1[l].astype(F32)))
               - jnp.exp(jnp.sum(lam_q2[l].astype(F32) * lam_k2[l].astype(F32))) + lam_init)
        y_a = gmlp_mixer(p[..., :o_a], gmlp_ws[l], gmlp_bs[l])
        y_b = diff_attention(p[..., o_a:o_q], p[..., o_q:o_k], p[..., o_k:o_v],
                             rel_bias, lam, diff_subln_g[l], lam_init)
        y_c = pool_mixer(p[..., o_v:], pool_w[l], pool_scale[l])
        h = jnp.concatenate([y_a, y_b, y_c], axis=-1) @ w_out[l]
        x = layer_norm(ALPHA * x + h, ln1_g[l], ln1_b[l])
        if l % 2 == 0:
            i = l // 2
            f = swiglu(x, ffn_w1[i], ffn_w3[i], ffn_w2[i])
        else:
            i = l // 2
            f = moe_ffn(x, router_w[i], moe_w1[i], moe_w3[i], moe_w2[i])
        x = layer_norm(ALPHA * x + f, ln2_g[l], ln2_b[l])
    return x
```

```python
import functools
import math

import jax
import jax.numpy as jnp
from jax import lax
from jax.experimental import pallas as pl
from jax.experimental.pallas import tpu as pltpu

F32 = jnp.float32
BF16 = jnp.bfloat16

H_A = 4
CHUNK = 128
H_B = 4
DH_B = 64
G_C = 4
POOL_WINDOWS = (2, 4, 8, 16)
POOL_HALO = 16
NUM_BUCKETS = 32
MAX_EXACT = NUM_BUCKETS // 2
MAX_DISTANCE = 128
N_EXPERTS = 8
EPS = 1e-5

LANES = 128
SUBLANES = 8
MIB = 1024 * 1024

ROW_TILE = 512
ATTN_BLOCK = 512
MOE_TILE = 512
COMBINE_TILE = 256
ROUTER_TILE = 512


def _cparams(semantics, vmem_mib):
    return pltpu.CompilerParams(dimension_semantics=semantics, vmem_limit_bytes=vmem_mib * MIB)


def _dot(a, b):
    return jnp.dot(a, b, preferred_element_type=F32)


def _layer_norm(r, g, b):
    mu = jnp.mean(r, axis=-1, keepdims=True)
    d = r - mu
    var = jnp.mean(d * d, axis=-1, keepdims=True)
    return d * lax.rsqrt(var + EPS) * g + b


def _split_bf16(a):
    hi = a.astype(BF16)
    lo = (a - hi.astype(F32)).astype(BF16)
    return hi, lo


def _in_proj_kernel(x_ref, w_ref, pf_ref, qkv_ref, *, n_f32):
    xb = x_ref[...].astype(BF16)
    pf_ref[...] = _dot(xb, w_ref[:, :n_f32])
    qkv_ref[...] = _dot(xb, w_ref[:, n_f32:]).astype(BF16)


def _in_proj(x, w, n_f32):
    t, d = x.shape
    n = w.shape[1]
    return pl.pallas_call(
        functools.partial(_in_proj_kernel, n_f32=n_f32),
        out_shape=(jax.ShapeDtypeStruct((t, n_f32), F32), jax.ShapeDtypeStruct((t, n - n_f32), BF16)),
        grid=(t // ROW_TILE,),
        in_specs=[pl.BlockSpec((ROW_TILE, d), lambda i: (i, 0)),
                  pl.BlockSpec((d, n), lambda i: (0, 0))],
        out_specs=(pl.BlockSpec((ROW_TILE, n_f32), lambda i: (i, 0)),
                   pl.BlockSpec((ROW_TILE, n - n_f32), lambda i: (i, 0))),
        compiler_params=_cparams(("arbitrary",), 48),
        name="in_proj",
    )(x, w)


def _mix_ac_kernel(z_ref, c_ref, cprev_ref, wcat_ref, bmat_ref, pw_ref, ps_ref,
                   ya_ref, yc_ref, ext_ref, *, seq_len):
    i = pl.program_id(0)
    tm, d_a2 = z_ref.shape
    d_a = d_a2 // 2
    hd = d_a // H_A
    d_c = c_ref.shape[1]
    gd = d_c // G_C

    z = z_ref[...]
    g = 0.5 * z * (1.0 + lax.erf(z * (1.0 / math.sqrt(2.0))))
    u = g[:, :d_a]
    v = g[:, d_a:]
    ri = lax.broadcasted_iota(jnp.int32, (d_a, d_a), 0) // hd
    ci = lax.broadcasted_iota(jnp.int32, (d_a, d_a), 1) // hd
    grp = jnp.where(ri == ci, 1.0, 0.0).astype(BF16)

    def group_mean(a):
        hi, lo = _split_bf16(a)
        return (_dot(hi, grp) + _dot(lo, grp)) * (1.0 / hd)

    mu = group_mean(v)
    dv = v - mu
    var = group_mean(dv * dv)
    vn = dv * lax.rsqrt(var + EPS)

    wr = lax.broadcasted_iota(jnp.int32, wcat_ref.shape, 0)
    wc = lax.broadcasted_iota(jnp.int32, wcat_ref.shape, 1) % CHUNK
    wcat = jnp.where(wc <= wr, wcat_ref[...], jnp.zeros_like(wcat_ref[...]))
    lane_head = lax.broadcasted_iota(jnp.int32, (CHUNK, d_a), 1) // hd
    bmat = bmat_ref[...]
    for ck in range(tm // CHUNK):
        rows = slice(ck * CHUNK, (ck + 1) * CHUNK)
        vn_c = vn[rows]
        stacked = jnp.concatenate(
            [jnp.where(lane_head == h, vn_c, 0.0).astype(BF16) for h in range(H_A)], axis=0)
        sg = _dot(wcat, stacked) + bmat
        ya_ref[rows, :] = (u[rows] * sg).astype(ya_ref.dtype)

    c = c_ref[...]
    pos0 = (i * tm) % seq_len
    prev = jnp.where(pos0 == 0, jnp.zeros_like(cprev_ref[...]), cprev_ref[...])
    ext_ref[0:POOL_HALO, :] = prev
    ext_ref[POOL_HALO:, :] = c
    lane_grp = lax.broadcasted_iota(jnp.int32, (tm, d_c), 1) // gd
    run = c
    wsum = jnp.zeros_like(c)
    win = jnp.zeros((tm, d_c), jnp.int32)
    k = 1
    for gi, w in enumerate(POOL_WINDOWS):
        while k < w:
            run = run + ext_ref[pl.ds(POOL_HALO - k, tm), :]
            k += 1
        wsum = jnp.where(lane_grp == gi, run, wsum)
        win = jnp.where(lane_grp == gi, w, win)
    pos = pos0 + lax.broadcasted_iota(jnp.int32, (tm, d_c), 0)
    cnt = jnp.minimum(pos + 1, win).astype(F32)
    y = (wsum / cnt - c).astype(BF16)
    yc_ref[...] = (_dot(y, pw_ref[...]) * ps_ref[...]).astype(yc_ref.dtype)


def _mix_ac(pf, wcat, bmat, pw, ps, seq_len, d_a, d_c):
    t = pf.shape[0]
    tm = ROW_TILE
    halo_blocks = tm // POOL_HALO
    c_col = (2 * d_a) // d_c
    return pl.pallas_call(
        functools.partial(_mix_ac_kernel, seq_len=seq_len),
        out_shape=(jax.ShapeDtypeStruct((t, d_a), BF16), jax.ShapeDtypeStruct((t, d_c), BF16)),
        grid=(t // tm,),
        in_specs=[pl.BlockSpec((tm, 2 * d_a), lambda i: (i, 0)),
                  pl.BlockSpec((tm, d_c), lambda i: (i, c_col)),
                  pl.BlockSpec((POOL_HALO, d_c), lambda i: (jnp.maximum(i * halo_blocks - 1, 0), c_col)),
                  pl.BlockSpec(wcat.shape, lambda i: (0, 0)),
                  pl.BlockSpec(bmat.shape, lambda i: (0, 0)),
                  pl.BlockSpec(pw.shape, lambda i: (0, 0)),
                  pl.BlockSpec(ps.shape, lambda i: (0, 0))],
        out_specs=(pl.BlockSpec((tm, d_a), lambda i: (i, 0)),
                   pl.BlockSpec((tm, d_c), lambda i: (i, 0))),
        scratch_shapes=[pltpu.VMEM((POOL_HALO + tm, d_c), F32)],
        compiler_params=_cparams(("arbitrary",), 48),
        name="mix_ac",
    )(pf, pf, pf, wcat, bmat, pw, ps)


def _attn_kernel(lam_ref, q_ref, k_ref, v_ref, bias_ref, g_ref, o_ref,
                 qs_ref, m_ref, l_ref, acc_ref):
    i = pl.program_id(2)
    tb, dv = q_ref.shape

    q = q_ref[...]
    lane = lax.broadcasted_iota(jnp.int32, (tb, dv), 1)
    qs_ref[0:tb, :] = jnp.where(lane < DH_B, q, jnp.zeros_like(q))
    qs_ref[tb:, :] = jnp.where(lane >= DH_B, q, jnp.zeros_like(q))
    m_ref[...] = jnp.full(m_ref.shape, -jnp.inf, F32)
    l_ref[...] = jnp.zeros(l_ref.shape, F32)
    acc_ref[...] = jnp.zeros(acc_ref.shape, F32)

    def step(j, bias):
        start = pl.multiple_of(j * tb, tb)
        kj = k_ref[pl.ds(start, tb), :]
        vj = v_ref[pl.ds(start, tb), :]
        s = lax.dot_general(qs_ref[...], kj, (((1,), (1,)), ((), ())), preferred_element_type=F32)
        if bias is not None:
            s = s + jnp.concatenate([bias, bias], axis=0)
        m_prev = m_ref[...]
        m_next = jnp.maximum(m_prev, jnp.max(s, axis=1, keepdims=True))
        alpha = jnp.exp2(m_prev - m_next)
        p = jnp.exp2(s - jnp.concatenate([m_next] * (tb // LANES), axis=1))
        l_ref[...] = alpha * l_ref[...] + jnp.sum(p, axis=1, keepdims=True)
        m_ref[...] = m_next
        acc_ref[...] = alpha * acc_ref[...] + _dot(p.astype(BF16), vj)

    def far_body(j, carry):
        step(j, None)
        return carry

    lax.fori_loop(0, jnp.maximum(i - 1, 0), far_body, 0)

    @pl.when(i >= 1)
    def _():
        step(i - 1, bias_ref[0])

    step(i, bias_ref[1])

    o = acc_ref[...] / l_ref[...]
    d = o[:tb] - lam_ref[0] * o[tb:]
    ms = jnp.mean(d * d, axis=-1, keepdims=True)
    o_ref[...] = (d * lax.rsqrt(ms + EPS) * g_ref[...]).astype(o_ref.dtype)


def _attention(qkv, bias, lam, gain, batch, seq_len):
    t = qkv.shape[0]
    tb = ATTN_BLOCK
    nq = seq_len // tb
    dv = 2 * DH_B
    return pl.pallas_call(
        _attn_kernel,
        out_shape=jax.ShapeDtypeStruct((t, H_B * dv), BF16),
        grid=(batch, H_B, nq),
        in_specs=[pl.BlockSpec(memory_space=pltpu.SMEM),
                  pl.BlockSpec((tb, dv), lambda b, h, i: (b * nq + i, h)),
                  pl.BlockSpec((seq_len, dv), lambda b, h, i: (b, H_B + h)),
                  pl.BlockSpec((seq_len, dv), lambda b, h, i: (b, 2 * H_B + h)),
                  pl.BlockSpec((None, 2, tb, tb), lambda b, h, i: (h, 0, 0, 0)),
                  pl.BlockSpec((1, dv), lambda b, h, i: (0, 0))],
        out_specs=pl.BlockSpec((tb, dv), lambda b, h, i: (b * nq + i, h)),
        scratch_shapes=[pltpu.VMEM((2 * tb, dv), BF16),
                        pltpu.VMEM((2 * tb, LANES), F32),
                        pltpu.VMEM((2 * tb, LANES), F32),
                        pltpu.VMEM((2 * tb, dv), F32)],
        compiler_params=_cparams(("arbitrary", "arbitrary", "arbitrary"), 56),
        name="diff_attn",
    )(lam, qkv, qkv, qkv, bias, gain)


def _t5_bucket(dist):
    n = jnp.maximum(dist, 0)
    nf = jnp.maximum(n, 1).astype(F32)
    large = MAX_EXACT + (jnp.log(nf / MAX_EXACT) / math.log(MAX_DISTANCE / MAX_EXACT)
                         * (NUM_BUCKETS - MAX_EXACT)).astype(jnp.int32)
    large = jnp.minimum(large, NUM_BUCKETS - 1)
    return jnp.where(n < MAX_EXACT, n, large)


def _attn_bias_tiles(rel_bias, tb):
    r = jnp.arange(tb)[:, None]
    c = jnp.arange(tb)[None, :]
    dist = jnp.stack([tb + r - c, r - c])
    rel = (rel_bias - rel_bias[NUM_BUCKETS - 1]) * math.log2(math.e)
    b = jnp.transpose(rel[_t5_bucket(dist)].astype(F32), (3, 0, 1, 2))
    return jnp.where(dist[None] >= 0, b, -jnp.inf)


def _out_proj_kernel(x_ref, ya_ref, yb_ref, yc_ref, w_ref, g_ref, b_ref, o_ref, *, alpha):
    d_a = ya_ref.shape[1]
    d_b = yb_ref.shape[1]
    h = (_dot(ya_ref[...], w_ref[0:d_a, :]) + _dot(yb_ref[...], w_ref[d_a:d_a + d_b, :])
         + _dot(yc_ref[...], w_ref[d_a + d_b:, :]))
    o_ref[...] = _layer_norm(alpha * x_ref[...] + h, g_ref[...], b_ref[...])


def _out_proj(x, ya, yb, yc, w, g, b, alpha):
    t, d = x.shape
    tm = ROW_TILE
    row = lambda n: pl.BlockSpec((tm, n), lambda i: (i, 0))
    full = lambda a: pl.BlockSpec(a.shape, lambda i: (0, 0))
    return pl.pallas_call(
        functools.partial(_out_proj_kernel, alpha=alpha),
        out_shape=jax.ShapeDtypeStruct((t, d), F32),
        grid=(t // tm,),
        in_specs=[row(d), row(ya.shape[1]), row(yb.shape[1]), row(yc.shape[1]), full(w), full(g), full(b)],
        out_specs=row(d),
        compiler_params=_cparams(("arbitrary",), 48),
        name="out_proj_ln",
    )(x, ya, yb, yc, w, g, b)


def _swiglu_chunk(xb, w1, w3, w2):
    h1 = _dot(xb, w1)
    h3 = _dot(xb, w3)
    gate = h1 * (1.0 / (1.0 + jnp.exp(-h1)))
    return _dot((gate * h3).astype(BF16), w2)


def _ffn_kernel(x_ref, w1_ref, w3_ref, w2_ref, g_ref, b_ref, o_ref, xb_ref, acc_ref, *, alpha):
    c = pl.program_id(1)

    @pl.when(c == 0)
    def _():
        xb_ref[...] = x_ref[...].astype(BF16)
        acc_ref[...] = jnp.zeros(acc_ref.shape, F32)

    acc_ref[...] += _swiglu_chunk(xb_ref[...], w1_ref[...], w3_ref[...], w2_ref[...])

    @pl.when(c == pl.num_programs(1) - 1)
    def _():
        o_ref[...] = _layer_norm(alpha * x_ref[...] + acc_ref[...], g_ref[...], b_ref[...])


def _ffn(x, w1, w3, w2, g, b, alpha, n_chunks):
    t, d = x.shape
    f = w1.shape[1]
    fc = f // n_chunks
    tm = ROW_TILE
    return pl.pallas_call(
        functools.partial(_ffn_kernel, alpha=alpha),
        out_shape=jax.ShapeDtypeStruct((t, d), F32),
        grid=(t // tm, n_chunks),
        in_specs=[pl.BlockSpec((tm, d), lambda i, c: (i, 0)),
                  pl.BlockSpec((d, fc), lambda i, c: (0, c)),
                  pl.BlockSpec((d, fc), lambda i, c: (0, c)),
                  pl.BlockSpec((fc, d), lambda i, c: (c, 0)),
                  pl.BlockSpec(g.shape, lambda i, c: (0, 0)),
                  pl.BlockSpec(b.shape, lambda i, c: (0, 0))],
        out_specs=pl.BlockSpec((tm, d), lambda i, c: (i, 0)),
        scratch_shapes=[pltpu.VMEM((tm, d), BF16), pltpu.VMEM((tm, d), F32)],
        compiler_params=_cparams(("arbitrary", "arbitrary"), 56),
        name="ffn_ln",
    )(x, w1, w3, w2, g, b)


def _router_kernel(x_ref, wh_ref, wl_ref, r_ref, cnt_ref, carry_ref):
    i = pl.program_id(0)
    tr = x_ref.shape[0]
    ne = N_EXPERTS

    @pl.when(i == 0)
    def _():
        carry_ref[...] = jnp.zeros(carry_ref.shape, F32)

    xh, xl = _split_bf16(x_ref[...])
    logits = _dot(xh, wh_ref[...]) + _dot(xl, wh_ref[...]) + _dot(xh, wl_ref[...])
    lt = jnp.transpose(logits)[0:ne, :]

    eio = lax.broadcasted_iota(jnp.int32, (ne, tr), 0)
    v1 = jnp.max(lt, axis=0, keepdims=True)
    i1 = jnp.min(jnp.where(lt == v1, eio, ne), axis=0, keepdims=True)
    oh1 = eio == i1
    lt2 = jnp.where(oh1, -jnp.inf, lt)
    v2 = jnp.max(lt2, axis=0, keepdims=True)
    i2 = jnp.min(jnp.where(lt2 == v2, eio, ne), axis=0, keepdims=True)
    oh2 = eio == i2
    e = jnp.exp(v2 - v1)
    g1 = 1.0 / (1.0 + e)
    g2 = e / (1.0 + e)

    member = jnp.where(oh1, 1.0, jnp.where(oh2, 1.0, 0.0))
    tri_r = lax.broadcasted_iota(jnp.int32, (tr, tr), 0)
    tri_c = lax.broadcasted_iota(jnp.int32, (tr, tr), 1)
    upper = jnp.where(tri_r < tri_c, 1.0, 0.0).astype(BF16)
    ranks = _dot(member.astype(BF16), upper) + carry_ref[:, 0:1]
    r1 = jnp.sum(jnp.where(oh1, ranks, 0.0), axis=0, keepdims=True)
    r2 = jnp.sum(jnp.where(oh2, ranks, 0.0), axis=0, keepdims=True)
    carry_ref[...] = carry_ref[...] + jnp.sum(member, axis=1, keepdims=True)
    cnt_ref[...] = carry_ref[...]

    rows = [i1.astype(F32), i2.astype(F32), g1, g2, r1, r2]
    out = jnp.zeros((ne, tr), F32)
    for k, row in enumerate(rows):
        out = jnp.where(eio == k, row, out)
    r_ref[...] = out


def _router(x, wh, wl):
    t, d = x.shape
    tr = ROUTER_TILE
    return pl.pallas_call(
        _router_kernel,
        out_shape=(jax.ShapeDtypeStruct((N_EXPERTS, t), F32), jax.ShapeDtypeStruct((N_EXPERTS, LANES), F32)),
        grid=(t // tr,),
        in_specs=[pl.BlockSpec((tr, d), lambda i: (i, 0)),
                  pl.BlockSpec(wh.shape, lambda i: (0, 0)),
                  pl.BlockSpec(wl.shape, lambda i: (0, 0))],
        out_specs=(pl.BlockSpec((N_EXPERTS, tr), lambda i: (0, i)),
                   pl.BlockSpec((N_EXPERTS, LANES), lambda i: (0, 0))),
        scratch_shapes=[pltpu.VMEM((N_EXPERTS, LANES), F32)],
        compiler_params=_cparams(("arbitrary",), 48),
        name="router",
    )(x, wh, wl)


def _gather_rows(idx_vmem, idx_smem, idx_sem, src_hbm, dst, row_sem, n_rows):
    cp = pltpu.make_async_copy(idx_vmem, idx_smem, idx_sem)
    cp.start()
    cp.wait()
    n_lists = idx_smem.shape[0]

    def body(r, carry):
        for k in range(n_lists):
            t = idx_smem[k, r]
            pltpu.make_async_copy(src_hbm.at[pl.ds(t, 1)], dst.at[k, pl.ds(r, 1)], row_sem).start()
        return carry

    lax.fori_loop(0, n_rows, body, 0, unroll=8)


def _moe_kernel(te_ref, na_ref, src_ref, srcn_ref, x_hbm, w1_ref, w3_ref, w2_ref, ys_ref,
                xbuf, xb_ref, acc_ref, idx_ref, row_sem, idx_sem):
    i = pl.program_id(0)
    c = pl.program_id(1)
    tm = xb_ref.shape[0]
    n_active = na_ref[0]
    slot = i % 2

    @pl.when(c == 0)
    def _():
        @pl.when(i == 0)
        def _():
            _gather_rows(src_ref.at[0], idx_ref, idx_sem, x_hbm, xbuf.at[0], row_sem.at[0], tm)

        @pl.when(i < n_active)
        def _():
            pltpu.make_async_copy(xbuf.at[slot], xbuf.at[slot], row_sem.at[slot]).wait()
            xb_ref[...] = xbuf[slot, 0].astype(BF16)
            acc_ref[...] = jnp.zeros(acc_ref.shape, F32)

        @pl.when(i + 1 < n_active)
        def _():
            _gather_rows(srcn_ref.at[0], idx_ref, idx_sem, x_hbm, xbuf.at[1 - slot], row_sem.at[1 - slot], tm)

    @pl.when(i < n_active)
    def _():
        acc_ref[...] += _swiglu_chunk(xb_ref[...], w1_ref[...], w3_ref[...], w2_ref[...])

    @pl.when(c == pl.num_programs(1) - 1)
    def _():
        ys_ref[...] = jnp.where(i < n_active, acc_ref[...], 0.0)


def _moe_experts(tile_expert, n_active, src, x, w1, w3, w2, n_chunks):
    n_tiles, _, tm = src.shape
    d = x.shape[1]
    f = w1.shape[2]
    fc = f // n_chunks

    def chunk(i, c, na):
        return jnp.where(i < na[0], c, n_chunks - 1)

    grid_spec = pltpu.PrefetchScalarGridSpec(
        num_scalar_prefetch=2,
        grid=(n_tiles, n_chunks),
        in_specs=[pl.BlockSpec((1, 1, tm), lambda i, c, te, na: (i, 0, 0)),
                  pl.BlockSpec((1, 1, tm), lambda i, c, te, na: (jnp.minimum(i + 1, n_tiles - 1), 0, 0)),
                  pl.BlockSpec(memory_space=pl.ANY),
                  pl.BlockSpec((None, d, fc), lambda i, c, te, na: (te[i], 0, chunk(i, c, na))),
                  pl.BlockSpec((None, d, fc), lambda i, c, te, na: (te[i], 0, chunk(i, c, na))),
                  pl.BlockSpec((None, fc, d), lambda i, c, te, na: (te[i], chunk(i, c, na), 0))],
        out_specs=pl.BlockSpec((tm, d), lambda i, c, te, na: (i, 0)),
        scratch_shapes=[pltpu.VMEM((2, 1, tm, d), F32),
                        pltpu.VMEM((tm, d), BF16),
                        pltpu.VMEM((tm, d), F32),
                        pltpu.SMEM((1, tm), jnp.int32),
                        pltpu.SemaphoreType.DMA((2,)),
                        pltpu.SemaphoreType.DMA(())],
    )
    return pl.pallas_call(
        _moe_kernel,
        out_shape=jax.ShapeDtypeStruct((n_tiles * tm, d), F32),
        grid_spec=grid_spec,
        compiler_params=_cparams(("arbitrary", "arbitrary"), 56),
        name="moe_experts",
    )(tile_expert, n_active, src, src, x, w1, w3, w2)


def _combine_kernel(pos_ref, posn_ref, gate_ref, x_ref, ys_hbm, g_ref, b_ref, o_ref,
                    ybuf, idx_ref, row_sem, idx_sem, *, alpha):
    i = pl.program_id(0)
    tc = x_ref.shape[0]
    slot = i % 2

    @pl.when(i == 0)
    def _():
        _gather_rows(pos_ref.at[0], idx_ref, idx_sem, ys_hbm, ybuf.at[0], row_sem.at[0], tc)

    pltpu.make_async_copy(ybuf.at[slot], ybuf.at[slot], row_sem.at[slot]).wait()

    @pl.when(i + 1 < pl.num_programs(0))
    def _():
        _gather_rows(posn_ref.at[0], idx_ref, idx_sem, ys_hbm, ybuf.at[1 - slot], row_sem.at[1 - slot], tc)

    gates = gate_ref[...]
    f = gates[:, 0:1] * ybuf[slot, 0] + gates[:, 1:2] * ybuf[slot, 1]
    o_ref[...] = _layer_norm(alpha * x_ref[...] + f, g_ref[...], b_ref[...])


def _combine(pos, gates, x, ys, g, b, alpha):
    t, d = x.shape
    tc = COMBINE_TILE
    n = t // tc
    return pl.pallas_call(
        functools.partial(_combine_kernel, alpha=alpha),
        out_shape=jax.ShapeDtypeStruct((t, d), F32),
        grid=(n,),
        in_specs=[pl.BlockSpec((1, 2, tc), lambda i: (i, 0, 0)),
                  pl.BlockSpec((1, 2, tc), lambda i: (jnp.minimum(i + 1, n - 1), 0, 0)),
                  pl.BlockSpec((tc, 2), lambda i: (i, 0)),
                  pl.BlockSpec((tc, d), lambda i: (i, 0)),
                  pl.BlockSpec(memory_space=pl.ANY),
                  pl.BlockSpec(g.shape, lambda i: (0, 0)),
                  pl.BlockSpec(b.shape, lambda i: (0, 0))],
        out_specs=pl.BlockSpec((tc, d), lambda i: (i, 0)),
        scratch_shapes=[pltpu.VMEM((2, 2, tc, d), F32),
                        pltpu.SMEM((2, tc), jnp.int32),
                        pltpu.SemaphoreType.DMA((2,)),
                        pltpu.SemaphoreType.DMA(())],
        compiler_params=_cparams(("arbitrary",), 48),
        name="moe_combine_ln",
    )(pos, pos, gates, x, ys, g, b)


def _moe_ffn(x, router_w, w1, w3, w2, g, b, alpha):
    t, d = x.shape
    tm = MOE_TILE
    rw = jnp.zeros((d, LANES), F32).at[:, :N_EXPERTS].set(router_w)
    wh = rw.astype(BF16)
    wl = (rw - wh.astype(F32)).astype(BF16)
    routed, counts = _router(x, wh, wl)

    e1 = routed[0].astype(jnp.int32)
    e2 = routed[1].astype(jnp.int32)
    r1 = routed[4].astype(jnp.int32)
    r2 = routed[5].astype(jnp.int32)
    cnt = counts[:, 0].astype(jnp.int32)
    tiles_per_expert = (cnt + tm - 1) // tm
    tile_end = jnp.cumsum(tiles_per_expert)
    row_start = (tile_end - tiles_per_expert) * tm
    pos1 = row_start[e1] + r1
    pos2 = row_start[e2] + r2
    n_tiles = (2 * t) // tm + N_EXPERTS
    n_active = tile_end[-1:]
    tile_expert = jnp.minimum(
        jnp.searchsorted(tile_end, jnp.minimum(jnp.arange(n_tiles), n_active[0] - 1), side="right"),
        N_EXPERTS - 1).astype(jnp.int32)
    tok = jnp.arange(t, dtype=jnp.int32)
    src = jnp.zeros((n_tiles * tm,), jnp.int32).at[pos1].set(tok).at[pos2].set(tok)

    ys = _moe_experts(tile_expert, n_active.astype(jnp.int32), src.reshape(n_tiles, 1, tm), x, w1, w3, w2, 4)

    tc = COMBINE_TILE
    pos = jnp.stack([pos1.reshape(t // tc, tc), pos2.reshape(t // tc, tc)], axis=1)
    gates = jnp.stack([routed[2], routed[3]], axis=1)
    return _combine(pos, gates, x, ys, g, b, alpha)


def kernel(x, w_in, w_out, gmlp_ws, gmlp_bs, lam_q1, lam_k1, lam_q2, lam_k2, diff_subln_g, rel_bias,
           pool_w, pool_scale, ln1_g, ln1_b, ln2_g, ln2_b, ffn_w1, ffn_w3, ffn_w2, router_w,
           moe_w1, moe_w3, moe_w2):
    batch, seq_len, d_model = x.shape
    depth = w_in.shape[0]
    d_a = w_out.shape[1] // 4
    d_b = H_B * 2 * DH_B
    d_c = pool_w.shape[1] * pool_w.shape[2]
    alpha = (2 * depth) ** 0.25
    t = batch * seq_len
    o_a = 2 * d_a
    o_q = o_a + d_b
    o_k = o_q + d_b
    o_v = o_k + d_b

    xt = x.reshape(t, d_model)
    bias = _attn_bias_tiles(rel_bias, ATTN_BLOCK)
    q_scale = DH_B ** -0.5 * math.log2(math.e)

    for l in range(depth):
        wl = w_in[l]
        w_cat = jnp.concatenate([wl[:, :o_a], wl[:, o_v:], wl[:, o_a:o_q] * q_scale, wl[:, o_q:o_v]],
                                axis=1).astype(BF16)
        wcat = jnp.transpose(gmlp_ws[l], (1, 0, 2)).reshape(CHUNK, H_A * CHUNK).astype(BF16)
        bmat = jnp.repeat(gmlp_bs[l].T, d_a // H_A, axis=1)
        pw = jax.scipy.linalg.block_diag(*[pool_w[l, gi] for gi in range(G_C)]).astype(BF16)
        ps = pool_scale[l].reshape(1, d_c)
        lam_init = 0.8 - 0.6 * math.exp(-0.3 * l)
        lam = (jnp.exp(jnp.sum(lam_q1[l] * lam_k1[l])) - jnp.exp(jnp.sum(lam_q2[l] * lam_k2[l]))
               + lam_init).reshape(1).astype(F32)
        gain = (diff_subln_g[l] * (1.0 - lam_init)).reshape(1, 2 * DH_B)

        pf, qkv = _in_proj(xt, w_cat, o_a + d_c)
        ya, yc = _mix_ac(pf, wcat, bmat, pw, ps, seq_len, d_a, d_c)
        yb = _attention(qkv, bias, lam, gain, batch, seq_len)
        xt = _out_proj(xt, ya, yb, yc, w_out[l].astype(BF16), ln1_g[l].reshape(1, -1), ln1_b[l].reshape(1, -1),
                       alpha)
        g2 = ln2_g[l].reshape(1, -1)
        b2 = ln2_b[l].reshape(1, -1)
        if l % 2 == 0:
            i = l // 2
            xt = _ffn(xt, ffn_w1[i].astype(BF16), ffn_w3[i].astype(BF16), ffn_w2[i].astype(BF16), g2, b2,
                      alpha, 2)
        else:
            i = l // 2
            xt = _moe_ffn(xt, router_w[i], moe_w1[i].astype(BF16), moe_w3[i].astype(BF16),
                          moe_w2[i].astype(BF16), g2, b2, alpha)
    return xt.reshape(batch, seq_len, d_model)
```

```python
import functools
import math

import jax
import jax.numpy as jnp
from jax import lax
from jax.experimental import pallas as pl
from jax.experimental.pallas import tpu as pltpu

F32 = jnp.float32
BF16 = jnp.bfloat16

H_A = 4
CHUNK = 128
H_B = 4
DH_B = 64
G_C = 4
POOL_WINDOWS = (2, 4, 8, 16)
POOL_HALO = 16
NUM_BUCKETS = 32
MAX_EXACT = NUM_BUCKETS // 2
MAX_DISTANCE = 128
N_EXPERTS = 8
EPS = 1e-5

LANES = 128
SUBLANES = 8
MIB = 1024 * 1024

ROW_TILE = 512
ATTN_BLOCK = 512
MOE_TILE = 512
COMBINE_TILE = 256
ROUTER_TILE = 512


def _cparams(semantics, vmem_mib):
    return pltpu.CompilerParams(dimension_semantics=semantics, vmem_limit_bytes=vmem_mib * MIB)


def _dot(a, b):
    return jnp.dot(a, b, preferred_element_type=F32)


def _layer_norm(r, g, b):
    mu = jnp.mean(r, axis=-1, keepdims=True)
    d = r - mu
    var = jnp.mean(d * d, axis=-1, keepdims=True)
    return d * lax.rsqrt(var + EPS) * g + b


def _split_bf16(a):
    hi = a.astype(BF16)
    lo = (a - hi.astype(F32)).astype(BF16)
    return hi, lo


def _in_proj_kernel(x_ref, w_ref, pf_ref, qkv_ref, *, n_f32):
    xb = x_ref[...].astype(BF16)
    pf_ref[...] = _dot(xb, w_ref[:, :n_f32])
    qkv_ref[...] = _dot(xb, w_ref[:, n_f32:]).astype(BF16)


def _in_proj(x, w, n_f32):
    t, d = x.shape
    n = w.shape[1]
    return pl.pallas_call(
        functools.partial(_in_proj_kernel, n_f32=n_f32),
        out_shape=(jax.ShapeDtypeStruct((t, n_f32), F32), jax.ShapeDtypeStruct((t, n - n_f32), BF16)),
        grid=(t // ROW_TILE,),
        in_specs=[pl.BlockSpec((ROW_TILE, d), lambda i: (i, 0)),
                  pl.BlockSpec((d, n), lambda i: (0, 0))],
        out_specs=(pl.BlockSpec((ROW_TILE, n_f32), lambda i: (i, 0)),
                   pl.BlockSpec((ROW_TILE, n - n_f32), lambda i: (i, 0))),
        compiler_params=_cparams(("arbitrary",), 48),
        name="in_proj",
    )(x, w)


def _mix_ac_kernel(z_ref, c_ref, cprev_ref, wcat_ref, bmat_ref, pw_ref, ps_ref,
                   ya_ref, yc_ref, ext_ref, *, seq_len):
    i = pl.program_id(0)
    tm, d_a2 = z_ref.shape
    d_a = d_a2 // 2
    hd = d_a // H_A
    d_c = c_ref.shape[1]
    gd = d_c // G_C

    z = z_ref[...]
    g = 0.5 * z * (1.0 + lax.erf(z * (1.0 / math.sqrt(2.0))))
    u = g[:, :d_a]
    v = g[:, d_a:]
    ri = lax.broadcasted_iota(jnp.int32, (d_a, d_a), 0) // hd
    ci = lax.broadcasted_iota(jnp.int32, (d_a, d_a), 1) // hd
    grp = jnp.where(ri == ci, 1.0, 0.0).astype(BF16)

    def group_mean(a):
        hi, lo = _split_bf16(a)
        return (_dot(hi, grp) + _dot(lo, grp)) * (1.0 / hd)

    mu = group_mean(v)
    dv = v - mu
    var = group_mean(dv * dv)
    vn = dv * lax.rsqrt(var + EPS)

    wr = lax.broadcasted_iota(jnp.int32, wcat_ref.shape, 0)
    wc = lax.broadcasted_iota(jnp.int32, wcat_ref.shape, 1) % CHUNK
    wcat = jnp.where(wc <= wr, wcat_ref[...], jnp.zeros_like(wcat_ref[...]))
    lane_head = lax.broadcasted_iota(jnp.int32, (CHUNK, d_a), 1) // hd
    bmat = bmat_ref[...]
    for ck in range(tm // CHUNK):
        rows = slice(ck * CHUNK, (ck + 1) * CHUNK)
        vn_c = vn[rows]
        stacked = jnp.concatenate(
            [jnp.where(lane_head == h, vn_c, 0.0).astype(BF16) for h in range(H_A)], axis=0)
        sg = _dot(wcat, stacked) + bmat
        ya_ref[rows, :] = (u[rows] * sg).astype(ya_ref.dtype)

    c = c_ref[...]
    pos0 = (i * tm) % seq_len
    prev = jnp.where(pos0 == 0, jnp.zeros_like(cprev_ref[...]), cprev_ref[...])
    ext_ref[0:POOL_HALO, :] = prev
    ext_ref[POOL_HALO:, :] = c
    lane_grp = lax.broadcasted_iota(jnp.int32, (tm, d_c), 1) // gd
    run = c
    wsum = jnp.zeros_like(c)
    win = jnp.zeros((tm, d_c), jnp.int32)
    k = 1
    for gi, w in enumerate(POOL_WINDOWS):
        while k < w:
            run = run + ext_ref[pl.ds(POOL_HALO - k, tm), :]
            k += 1
        wsum = jnp.where(lane_grp == gi, run, wsum)
        win = jnp.where(lane_grp == gi, w, win)
    pos = pos0 + lax.broadcasted_iota(jnp.int32, (tm, d_c), 0)
    cnt = jnp.minimum(pos + 1, win).astype(F32)
    y = (wsum / cnt - c).astype(BF16)
    yc_ref[...] = (_dot(y, pw_ref[...]) * ps_ref[...]).astype(yc_ref.dtype)


def _mix_ac(pf, wcat, bmat, pw, ps, seq_len, d_a, d_c):
    t = pf.shape[0]
    tm = ROW_TILE
    halo_blocks = tm // POOL_HALO
    c_col = (2 * d_a) // d_c
    return pl.pallas_call(
        functools.partial(_mix_ac_kernel, seq_len=seq_len),
        out_shape=(jax.ShapeDtypeStruct((t, d_a), BF16), jax.ShapeDtypeStruct((t, d_c), BF16)),
        grid=(t // tm,),
        in_specs=[pl.BlockSpec((tm, 2 * d_a), lambda i: (i, 0)),
                  pl.BlockSpec((tm, d_c), lambda i: (i, c_col)),
                  pl.BlockSpec((POOL_HALO, d_c), lambda i: (jnp.maximum(i * halo_blocks - 1, 0), c_col)),
                  pl.BlockSpec(wcat.shape, lambda i: (0, 0)),
                  pl.BlockSpec(bmat.shape, lambda i: (0, 0)),
                  pl.BlockSpec(pw.shape, lambda i: (0, 0)),
                  pl.BlockSpec(ps.shape, lambda i: (0, 0))],
        out_specs=(pl.BlockSpec((tm, d_a), lambda i: (i, 0)),
                   pl.BlockSpec((tm, d_c), lambda i: (i, 0))),
        scratch_shapes=[pltpu.VMEM((POOL_HALO + tm, d_c), F32)],
        compiler_params=_cparams(("arbitrary",), 48),
        name="mix_ac",
    )(pf, pf, pf, wcat, bmat, pw, ps)


def _attn_kernel(lam_ref, q_ref, k_ref, v_ref, bias_ref, g_ref, o_ref,
                 qs_ref, m_ref, l_ref, acc_ref, s0_ref, s1_ref, c0_ref, c1_ref):
    i = pl.program_id(2)
    tb, dv = q_ref.shape

    q = q_ref[...]
    lane = lax.broadcasted_iota(jnp.int32, (tb, dv), 1)
    qs_ref[0:tb, :] = jnp.where(lane < DH_B, q, jnp.zeros_like(q))
    qs_ref[tb:, :] = jnp.where(lane >= DH_B, q, jnp.zeros_like(q))
    m_ref[...] = jnp.full(m_ref.shape, -jnp.inf, F32)
    l_ref[...] = jnp.zeros(l_ref.shape, F32)
    acc_ref[...] = jnp.zeros(acc_ref.shape, F32)

    bufs = ((s0_ref, c0_ref), (s1_ref, c1_ref))

    def scores(buf, j, bias=None):
        s_ref, cm_ref = buf
        kj = k_ref[pl.ds(pl.multiple_of(j * tb, tb), tb), :]
        s = lax.dot_general(kj, qs_ref[...], (((1,), (1,)), ((), ())), preferred_element_type=F32)
        if bias is not None:
            s = s + jnp.concatenate([bias, bias], axis=1)
        s_ref[...] = s
        cm_ref[...] = jnp.max(s, axis=0, keepdims=True)

    def consume(buf, j, bias=None):
        s_ref, cm_ref = buf
        vjt = jnp.transpose(v_ref[pl.ds(pl.multiple_of(j * tb, tb), tb), :])
        s = s_ref[...]
        cm = cm_ref[...]
        if bias is not None:
            s = s + jnp.concatenate([bias, bias], axis=1)
            cm = jnp.max(s, axis=0, keepdims=True)
        m_prev = m_ref[...]
        m_next = jnp.maximum(m_prev, cm)
        alpha = jnp.exp2(m_prev - m_next)
        p = jnp.exp2(s - m_next)
        l_ref[...] = alpha * l_ref[...] + jnp.sum(p, axis=0, keepdims=True)
        m_ref[...] = m_next
        acc_ref[...] = alpha * acc_ref[...] + _dot(vjt, p.astype(BF16))

    def pipelined(base, n):
        for u in range(n):
            scores(bufs[(u + 1) % 2], base + u + 1)
            consume(bufs[u % 2], base + u)

    n_far = jnp.maximum(i - 1, 0)

    @pl.when(i == 0)
    def _():
        scores(bufs[0], 0, bias_ref[1])
        consume(bufs[0], 0)

    @pl.when(i == 1)
    def _():
        scores(bufs[0], 0, bias_ref[0])
        scores(bufs[1], 1, bias_ref[1])
        consume(bufs[0], 0)
        consume(bufs[1], 1)

    @pl.when(i >= 2)
    def _():
        scores(bufs[0], 0)

        def quad_body(t, carry):
            pipelined(4 * t, 4)
            return carry

        lax.fori_loop(0, n_far // 4, quad_body, 0)

        def pair_body(t, carry):
            pipelined((n_far // 4) * 4 + 2 * t, 2)
            return carry

        lax.fori_loop(0, (n_far % 4) // 2, pair_body, 0)

        @pl.when(n_far % 2 == 0)
        def _():
            scores(bufs[1], i, bias_ref[1])
            consume(bufs[0], i - 1, bias_ref[0])
            consume(bufs[1], i)

        @pl.when(n_far % 2 == 1)
        def _():
            scores(bufs[1], i - 1, bias_ref[0])
            consume(bufs[0], i - 2)
            scores(bufs[0], i, bias_ref[1])
            consume(bufs[1], i - 1)
            consume(bufs[0], i)

    o = acc_ref[...] / l_ref[...]
    d = o[:, :tb] - lam_ref[0] * o[:, tb:]
    ms = jnp.mean(d * d, axis=0, keepdims=True)
    y = d * lax.rsqrt(ms + EPS) * g_ref[...]
    o_ref[...] = jnp.transpose(y).astype(o_ref.dtype)


def _attention(qkv, bias, lam, gain, batch, seq_len):
    t = qkv.shape[0]
    tb = ATTN_BLOCK
    nq = seq_len // tb
    dv = 2 * DH_B
    return pl.pallas_call(
        _attn_kernel,
        out_shape=jax.ShapeDtypeStruct((t, H_B * dv), BF16),
        grid=(batch, H_B, nq),
        in_specs=[pl.BlockSpec(memory_space=pltpu.SMEM),
                  pl.BlockSpec((tb, dv), lambda b, h, i: (b * nq + i, h)),
                  pl.BlockSpec((seq_len, dv), lambda b, h, i: (b, H_B + h)),
                  pl.BlockSpec((seq_len, dv), lambda b, h, i: (b, 2 * H_B + h)),
                  pl.BlockSpec((None, 2, tb, tb), lambda b, h, i: (h, 0, 0, 0)),
                  pl.BlockSpec((dv, 1), lambda b, h, i: (0, 0))],
        out_specs=pl.BlockSpec((tb, dv), lambda b, h, i: (b * nq + i, h)),
        scratch_shapes=[pltpu.VMEM((2 * tb, dv), BF16),
                        pltpu.VMEM((1, 2 * tb), F32),
                        pltpu.VMEM((1, 2 * tb), F32),
                        pltpu.VMEM((dv, 2 * tb), F32),
                        pltpu.VMEM((tb, 2 * tb), F32),
                        pltpu.VMEM((tb, 2 * tb), F32),
                        pltpu.VMEM((1, 2 * tb), F32),
                        pltpu.VMEM((1, 2 * tb), F32)],
        compiler_params=_cparams(("arbitrary", "arbitrary", "arbitrary"), 56),
        name="diff_attn",
    )(lam, qkv, qkv, qkv, bias, gain)


def _bias_kernel(rel_ref, o_ref):
    h = pl.program_id(0)
    tb = o_ref.shape[-1]
    kk = lax.broadcasted_iota(jnp.int32, (tb, tb), 0)
    qq = lax.broadcasted_iota(jnp.int32, (tb, tb), 1)
    far = rel_ref[(NUM_BUCKETS - 1) * H_B + h]
    for w, off in enumerate((tb, 0)):
        dist = off + qq - kk
        n = jnp.maximum(dist, 0)
        nf = jnp.maximum(n, 1).astype(F32)
        large = MAX_EXACT + (jnp.log(nf / MAX_EXACT) / math.log(MAX_DISTANCE / MAX_EXACT)
                             * (NUM_BUCKETS - MAX_EXACT)).astype(jnp.int32)
        bucket = jnp.where(n < MAX_EXACT, n, jnp.minimum(large, NUM_BUCKETS - 1))
        b = jnp.zeros((tb, tb), F32)
        for k in range(NUM_BUCKETS):
            b = jnp.where(bucket == k, rel_ref[k * H_B + h], b)
        o_ref[w] = jnp.where(dist >= 0, (b - far) * math.log2(math.e), -jnp.inf)


def _attn_bias_tiles(rel_bias, tb):
    return pl.pallas_call(
        _bias_kernel,
        out_shape=jax.ShapeDtypeStruct((H_B, 2, tb, tb), F32),
        grid=(H_B,),
        in_specs=[pl.BlockSpec(memory_space=pltpu.SMEM)],
        out_specs=pl.BlockSpec((None, 2, tb, tb), lambda h: (h, 0, 0, 0)),
        compiler_params=_cparams(("arbitrary",), 48),
        name="attn_bias",
    )(rel_bias.reshape(NUM_BUCKETS * H_B))


def _out_proj_kernel(x_ref, ya_ref, yb_ref, yc_ref, w_ref, g_ref, b_ref, o_ref, *, alpha):
    d_a = ya_ref.shape[1]
    d_b = yb_ref.shape[1]
    h = (_dot(ya_ref[...], w_ref[0:d_a, :]) + _dot(yb_ref[...], w_ref[d_a:d_a + d_b, :])
         + _dot(yc_ref[...], w_ref[d_a + d_b:, :]))
    o_ref[...] = _layer_norm(alpha * x_ref[...] + h, g_ref[...], b_ref[...])


def _out_proj(x, ya, yb, yc, w, g, b, alpha):
    t, d = x.shape
    tm = ROW_TILE
    row = lambda n: pl.BlockSpec((tm, n), lambda i: (i, 0))
    full = lambda a: pl.BlockSpec(a.shape, lambda i: (0, 0))
    return pl.pallas_call(
        functools.partial(_out_proj_kernel, alpha=alpha),
        out_shape=jax.ShapeDtypeStruct((t, d), F32),
        grid=(t // tm,),
        in_specs=[row(d), row(ya.shape[1]), row(yb.shape[1]), row(yc.shape[1]), full(w), full(g), full(b)],
        out_specs=row(d),
        compiler_params=_cparams(("arbitrary",), 48),
        name="out_proj_ln",
    )(x, ya, yb, yc, w, g, b)


def _swiglu_chunk(xb, w1, w3, w2):
    h1 = _dot(xb, w1)
    h3 = _dot(xb, w3)
    gate = h1 * (1.0 / (1.0 + jnp.exp(-h1)))
    return _dot((gate * h3).astype(BF16), w2)


def _ffn_kernel(x_ref, w1_ref, w3_ref, w2_ref, g_ref, b_ref, o_ref, xb_ref, acc_ref, *, alpha):
    c = pl.program_id(1)

    @pl.when(c == 0)
    def _():
        xb_ref[...] = x_ref[...].astype(BF16)
        acc_ref[...] = jnp.zeros(acc_ref.shape, F32)

    acc_ref[...] += _swiglu_chunk(xb_ref[...], w1_ref[...], w3_ref[...], w2_ref[...])

    @pl.when(c == pl.num_programs(1) - 1)
    def _():
        o_ref[...] = _layer_norm(alpha * x_ref[...] + acc_ref[...], g_ref[...], b_ref[...])


def _ffn(x, w1, w3, w2, g, b, alpha, n_chunks):
    t, d = x.shape
    f = w1.shape[1]
    fc = f // n_chunks
    tm = ROW_TILE
    return pl.pallas_call(
        functools.partial(_ffn_kernel, alpha=alpha),
        out_shape=jax.ShapeDtypeStruct((t, d), F32),
        grid=(t // tm, n_chunks),
        in_specs=[pl.BlockSpec((tm, d), lambda i, c: (i, 0)),
                  pl.BlockSpec((d, fc), lambda i, c: (0, c)),
                  pl.BlockSpec((d, fc), lambda i, c: (0, c)),
                  pl.BlockSpec((fc, d), lambda i, c: (c, 0)),
                  pl.BlockSpec(g.shape, lambda i, c: (0, 0)),
                  pl.BlockSpec(b.shape, lambda i, c: (0, 0))],
        out_specs=pl.BlockSpec((tm, d), lambda i, c: (i, 0)),
        scratch_shapes=[pltpu.VMEM((tm, d), BF16), pltpu.VMEM((tm, d), F32)],
        compiler_params=_cparams(("arbitrary", "arbitrary"), 56),
        name="ffn_ln",
    )(x, w1, w3, w2, g, b)


def _router_kernel(x_ref, wh_ref, wl_ref, r_ref, cnt_ref, carry_ref):
    i = pl.program_id(0)
    tr = x_ref.shape[0]
    ne = N_EXPERTS

    @pl.when(i == 0)
    def _():
        carry_ref[...] = jnp.zeros(carry_ref.shape, F32)

    xh, xl = _split_bf16(x_ref[...])
    logits = _dot(xh, wh_ref[...]) + _dot(xl, wh_ref[...]) + _dot(xh, wl_ref[...])
    lt = jnp.transpose(logits)[0:ne, :]

    eio = lax.broadcasted_iota(jnp.int32, (ne, tr), 0)
    v1 = jnp.max(lt, axis=0, keepdims=True)
    i1 = jnp.min(jnp.where(lt == v1, eio, ne), axis=0, keepdims=True)
    oh1 = eio == i1
    lt2 = jnp.where(oh1, -jnp.inf, lt)
    v2 = jnp.max(lt2, axis=0, keepdims=True)
    i2 = jnp.min(jnp.where(lt2 == v2, eio, ne), axis=0, keepdims=True)
    oh2 = eio == i2
    e = jnp.exp(v2 - v1)
    g1 = 1.0 / (1.0 + e)
    g2 = e / (1.0 + e)

    member = jnp.where(oh1, 1.0, jnp.where(oh2, 1.0, 0.0))
    tri_r = lax.broadcasted_iota(jnp.int32, (tr, tr), 0)
    tri_c = lax.broadcasted_iota(jnp.int32, (tr, tr), 1)
    upper = jnp.where(tri_r < tri_c, 1.0, 0.0).astype(BF16)
    ranks = _dot(member.astype(BF16), upper) + carry_ref[:, 0:1]
    r1 = jnp.sum(jnp.where(oh1, ranks, 0.0), axis=0, keepdims=True)
    r2 = jnp.sum(jnp.where(oh2, ranks, 0.0), axis=0, keepdims=True)
    carry_ref[...] = carry_ref[...] + jnp.sum(member, axis=1, keepdims=True)
    cnt_ref[...] = carry_ref[...]

    rows = [i1.astype(F32), i2.astype(F32), g1, g2, r1, r2]
    out = jnp.zeros((ne, tr), F32)
    for k, row in enumerate(rows):
        out = jnp.where(eio == k, row, out)
    r_ref[...] = out


def _router(x, wh, wl):
    t, d = x.shape
    tr = ROUTER_TILE
    return pl.pallas_call(
        _router_kernel,
        out_shape=(jax.ShapeDtypeStruct((N_EXPERTS, t), F32), jax.ShapeDtypeStruct((N_EXPERTS, LANES), F32)),
        grid=(t // tr,),
        in_specs=[pl.BlockSpec((tr, d), lambda i: (i, 0)),
                  pl.BlockSpec(wh.shape, lambda i: (0, 0)),
                  pl.BlockSpec(wl.shape, lambda i: (0, 0))],
        out_specs=(pl.BlockSpec((N_EXPERTS, tr), lambda i: (0, i)),
                   pl.BlockSpec((N_EXPERTS, LANES), lambda i: (0, 0))),
        scratch_shapes=[pltpu.VMEM((N_EXPERTS, LANES), F32)],
        compiler_params=_cparams(("arbitrary",), 48),
        name="router",
    )(x, wh, wl)


def _gather_rows(idx_vmem, idx_smem, idx_sem, src_hbm, dst, row_sem, n_rows):
    cp = pltpu.make_async_copy(idx_vmem, idx_smem, idx_sem)
    cp.start()
    cp.wait()
    n_lists = idx_smem.shape[0]

    def body(r, carry):
        for k in range(n_lists):
            t = idx_smem[k, r]
            pltpu.make_async_copy(src_hbm.at[pl.ds(t, 1)], dst.at[k, pl.ds(r, 1)], row_sem).start()
        return carry

    lax.fori_loop(0, n_rows, body, 0, unroll=8)


def _moe_kernel(te_ref, na_ref, src_ref, srcn_ref, x_hbm, w1_ref, w3_ref, w2_ref, ys_ref,
                xbuf, xb_ref, acc_ref, idx_ref, row_sem, idx_sem):
    i = pl.program_id(0)
    c = pl.program_id(1)
    tm = xb_ref.shape[0]
    n_active = na_ref[0]
    slot = i % 2

    @pl.when(c == 0)
    def _():
        @pl.when(i == 0)
        def _():
            _gather_rows(src_ref.at[0], idx_ref, idx_sem, x_hbm, xbuf.at[0], row_sem.at[0], tm)

        @pl.when(i < n_active)
        def _():
            pltpu.make_async_copy(xbuf.at[slot], xbuf.at[slot], row_sem.at[slot]).wait()
            xb_ref[...] = xbuf[slot, 0].astype(BF16)
            acc_ref[...] = jnp.zeros(acc_ref.shape, F32)

        @pl.when(i + 1 < n_active)
        def _():
            _gather_rows(srcn_ref.at[0], idx_ref, idx_sem, x_hbm, xbuf.at[1 - slot], row_sem.at[1 - slot], tm)

    @pl.when(i < n_active)
    def _():
        acc_ref[...] += _swiglu_chunk(xb_ref[...], w1_ref[...], w3_ref[...], w2_ref[...])

    @pl.when(c == pl.num_programs(1) - 1)
    def _():
        ys_ref[...] = jnp.where(i < n_active, acc_ref[...], 0.0)


def _moe_experts(tile_expert, n_active, src, x, w1, w3, w2, n_chunks):
    n_tiles, _, tm = src.shape
    d = x.shape[1]
    f = w1.shape[2]
    fc = f // n_chunks

    def chunk(i, c, na):
        return jnp.where(i < na[0], c, n_chunks - 1)

    grid_spec = pltpu.PrefetchScalarGridSpec(
        num_scalar_prefetch=2,
        grid=(n_tiles, n_chunks),
        in_specs=[pl.BlockSpec((1, 1, tm), lambda i, c, te, na: (i, 0, 0)),
                  pl.BlockSpec((1, 1, tm), lambda i, c, te, na: (jnp.minimum(i + 1, n_tiles - 1), 0, 0)),
                  pl.BlockSpec(memory_space=pl.ANY),
                  pl.BlockSpec((None, d, fc), lambda i, c, te, na: (te[i], 0, chunk(i, c, na))),
                  pl.BlockSpec((None, d, fc), lambda i, c, te, na: (te[i], 0, chunk(i, c, na))),
                  pl.BlockSpec((None, fc, d), lambda i, c, te, na: (te[i], chunk(i, c, na), 0))],
        out_specs=pl.BlockSpec((tm, d), lambda i, c, te, na: (i, 0)),
        scratch_shapes=[pltpu.VMEM((2, 1, tm, d), F32),
                        pltpu.VMEM((tm, d), BF16),
                        pltpu.VMEM((tm, d), F32),
                        pltpu.SMEM((1, tm), jnp.int32),
                        pltpu.SemaphoreType.DMA((2,)),
                        pltpu.SemaphoreType.DMA(())],
    )
    return pl.pallas_call(
        _moe_kernel,
        out_shape=jax.ShapeDtypeStruct((n_tiles * tm, d), F32),
        grid_spec=grid_spec,
        compiler_params=_cparams(("arbitrary", "arbitrary"), 56),
        name="moe_experts",
    )(tile_expert, n_active, src, src, x, w1, w3, w2)


def _combine_kernel(pos_ref, posn_ref, gate_ref, x_ref, ys_hbm, g_ref, b_ref, o_ref,
                    ybuf, idx_ref, row_sem, idx_sem, *, alpha):
    i = pl.program_id(0)
    tc = x_ref.shape[0]
    slot = i % 2

    @pl.when(i == 0)
    def _():
        _gather_rows(pos_ref.at[0], idx_ref, idx_sem, ys_hbm, ybuf.at[0], row_sem.at[0], tc)

    pltpu.make_async_copy(ybuf.at[slot], ybuf.at[slot], row_sem.at[slot]).wait()

    @pl.when(i + 1 < pl.num_programs(0))
    def _():
        _gather_rows(posn_ref.at[0], idx_ref, idx_sem, ys_hbm, ybuf.at[1 - slot], row_sem.at[1 - slot], tc)

    gates = gate_ref[...]
    f = gates[:, 0:1] * ybuf[slot, 0] + gates[:, 1:2] * ybuf[slot, 1]
    o_ref[...] = _layer_norm(alpha * x_ref[...] + f, g_ref[...], b_ref[...])


def _combine(pos, gates, x, ys, g, b, alpha):
    t, d = x.shape
    tc = COMBINE_TILE
    n = t // tc
    return pl.pallas_call(
        functools.partial(_combine_kernel, alpha=alpha),
        out_shape=jax.ShapeDtypeStruct((t, d), F32),
        grid=(n,),
        in_specs=[pl.BlockSpec((1, 2, tc), lambda i: (i, 0, 0)),
                  pl.BlockSpec((1, 2, tc), lambda i: (jnp.minimum(i + 1, n - 1), 0, 0)),
                  pl.BlockSpec((tc, 2), lambda i: (i, 0)),
                  pl.BlockSpec((tc, d), lambda i: (i, 0)),
                  pl.BlockSpec(memory_space=pl.ANY),
                  pl.BlockSpec(g.shape, lambda i: (0, 0)),
                  pl.BlockSpec(b.shape, lambda i: (0, 0))],
        out_specs=pl.BlockSpec((tc, d), lambda i: (i, 0)),
        scratch_shapes=[pltpu.VMEM((2, 2, tc, d), F32),
                        pltpu.SMEM((2, tc), jnp.int32),
                        pltpu.SemaphoreType.DMA((2,)),
                        pltpu.SemaphoreType.DMA(())],
        compiler_params=_cparams(("arbitrary",), 48),
        name="moe_combine_ln",
    )(pos, pos, gates, x, ys, g, b)


def _moe_ffn(x, router_w, w1, w3, w2, g, b, alpha):
    t, d = x.shape
    tm = MOE_TILE
    rw = jnp.zeros((d, LANES), F32).at[:, :N_EXPERTS].set(router_w)
    wh = rw.astype(BF16)
    wl = (rw - wh.astype(F32)).astype(BF16)
    routed, counts = _router(x, wh, wl)

    e1 = routed[0].astype(jnp.int32)
    e2 = routed[1].astype(jnp.int32)
    r1 = routed[4].astype(jnp.int32)
    r2 = routed[5].astype(jnp.int32)
    cnt = counts[:, 0].astype(jnp.int32)
    tiles_per_expert = (cnt + tm - 1) // tm
    tile_end = jnp.cumsum(tiles_per_expert)
    row_start = (tile_end - tiles_per_expert) * tm
    pos1 = row_start[e1] + r1
    pos2 = row_start[e2] + r2
    n_tiles = (2 * t) // tm + N_EXPERTS
    n_active = tile_end[-1:]
    tile_expert = jnp.minimum(
        jnp.searchsorted(tile_end, jnp.minimum(jnp.arange(n_tiles), n_active[0] - 1), side="right"),
        N_EXPERTS - 1).astype(jnp.int32)
    tok = jnp.arange(t, dtype=jnp.int32)
    src = jnp.zeros((n_tiles * tm,), jnp.int32).at[pos1].set(tok).at[pos2].set(tok)

    ys = _moe_experts(tile_expert, n_active.astype(jnp.int32), src.reshape(n_tiles, 1, tm), x, w1, w3, w2, 4)

    tc = COMBINE_TILE
    pos = jnp.stack([pos1.reshape(t // tc, tc), pos2.reshape(t // tc, tc)], axis=1)
    gates = jnp.stack([routed[2], routed[3]], axis=1)
    return _combine(pos, gates, x, ys, g, b, alpha)


def kernel(x, w_in, w_out, gmlp_ws, gmlp_bs, lam_q1, lam_k1, lam_q2, lam_k2, diff_subln_g, rel_bias,
           pool_w, pool_scale, ln1_g, ln1_b, ln2_g, ln2_b, ffn_w1, ffn_w3, ffn_w2, router_w,
           moe_w1, moe_w3, moe_w2):
    batch, seq_len, d_model = x.shape
    depth = w_in.shape[0]
    d_a = w_out.shape[1] // 4
    d_b = H_B * 2 * DH_B
    d_c = pool_w.shape[1] * pool_w.shape[2]
    alpha = (2 * depth) ** 0.25
    t = batch * seq_len
    o_a = 2 * d_a
    o_q = o_a + d_b
    o_k = o_q + d_b
    o_v = o_k + d_b

    xt = x.reshape(t, d_model)
    bias = _attn_bias_tiles(rel_bias, ATTN_BLOCK)
    q_scale = DH_B ** -0.5 * math.log2(math.e)

    for l in range(depth):
        wl = w_in[l]
        w_cat = jnp.concatenate([wl[:, :o_a], wl[:, o_v:], wl[:, o_a:o_q] * q_scale, wl[:, o_q:o_v]],
                                axis=1).astype(BF16)
        wcat = jnp.transpose(gmlp_ws[l], (1, 0, 2)).reshape(CHUNK, H_A * CHUNK).astype(BF16)
        bmat = jnp.repeat(gmlp_bs[l].T, d_a // H_A, axis=1)
        pw = jax.scipy.linalg.block_diag(*[pool_w[l, gi] for gi in range(G_C)]).astype(BF16)
        ps = pool_scale[l].reshape(1, d_c)
        lam_init = 0.8 - 0.6 * math.exp(-0.3 * l)
        lam = (jnp.exp(jnp.sum(lam_q1[l] * lam_k1[l])) - jnp.exp(jnp.sum(lam_q2[l] * lam_k2[l]))
               + lam_init).reshape(1).astype(F32)
        gain = (diff_subln_g[l] * (1.0 - lam_init)).reshape(2 * DH_B, 1)

        pf, qkv = _in_proj(xt, w_cat, o_a + d_c)
        ya, yc = _mix_ac(pf, wcat, bmat, pw, ps, seq_len, d_a, d_c)
        yb = _attention(qkv, bias, lam, gain, batch, seq_len)
        xt = _out_proj(xt, ya, yb, yc, w_out[l].astype(BF16), ln1_g[l].reshape(1, -1), ln1_b[l].reshape(1, -1),
                       alpha)
        g2 = ln2_g[l].reshape(1, -1)
        b2 = ln2_b[l].reshape(1, -1)
        if l % 2 == 0:
            i = l // 2
            xt = _ffn(xt, ffn_w1[i].astype(BF16), ffn_w3[i].astype(BF16), ffn_w2[i].astype(BF16), g2, b2,
                      alpha, 2)
        else:
            i = l // 2
            xt = _moe_ffn(xt, router_w[i], moe_w1[i].astype(BF16), moe_w3[i].astype(BF16),
                          moe_w2[i].astype(BF16), g2, b2, alpha)
    return xt.reshape(batch, seq_len, d_model)
```

```python
import functools
import math

import jax
import jax.numpy as jnp
from jax import lax
from jax.experimental import pallas as pl
from jax.experimental.pallas import tpu as pltpu

F32 = jnp.float32
BF16 = jnp.bfloat16

H_A = 4
CHUNK = 128
H_B = 4
DH_B = 64
G_C = 4
POOL_WINDOWS = (2, 4, 8, 16)
POOL_HALO = 16
NUM_BUCKETS = 32
MAX_EXACT = NUM_BUCKETS // 2
MAX_DISTANCE = 128
N_EXPERTS = 8
EPS = 1e-5

LANES = 128
SUBLANES = 8
MIB = 1024 * 1024

ROW_TILE = 512
ATTN_BLOCK = 512
MOE_TILE = 1024
COMBINE_TILE = 256
ROUTER_TILE = 512


def _cparams(semantics, vmem_mib):
    return pltpu.CompilerParams(dimension_semantics=semantics, vmem_limit_bytes=vmem_mib * MIB)


def _dot(a, b):
    return jnp.dot(a, b, preferred_element_type=F32)


def _layer_norm(r, g, b):
    mu = jnp.mean(r, axis=-1, keepdims=True)
    d = r - mu
    var = jnp.mean(d * d, axis=-1, keepdims=True)
    return d * lax.rsqrt(var + EPS) * g + b


def _split_bf16(a):
    hi = a.astype(BF16)
    lo = (a - hi.astype(F32)).astype(BF16)
    return hi, lo


def _in_proj_kernel(x_ref, w_ref, pf_ref, qkv_ref, *, n_f32):
    xb = x_ref[...].astype(BF16)
    pf_ref[...] = _dot(xb, w_ref[:, :n_f32])
    qkv_ref[...] = _dot(xb, w_ref[:, n_f32:]).astype(BF16)


def _in_proj(x, w, n_f32):
    t, d = x.shape
    n = w.shape[1]
    return pl.pallas_call(
        functools.partial(_in_proj_kernel, n_f32=n_f32),
        out_shape=(jax.ShapeDtypeStruct((t, n_f32), F32), jax.ShapeDtypeStruct((t, n - n_f32), BF16)),
        grid=(t // ROW_TILE,),
        in_specs=[pl.BlockSpec((ROW_TILE, d), lambda i: (i, 0)),
                  pl.BlockSpec((d, n), lambda i: (0, 0))],
        out_specs=(pl.BlockSpec((ROW_TILE, n_f32), lambda i: (i, 0)),
                   pl.BlockSpec((ROW_TILE, n - n_f32), lambda i: (i, 0))),
        compiler_params=_cparams(("arbitrary",), 48),
        name="in_proj",
    )(x, w)


def _mix_ac_kernel(z_ref, c_ref, cprev_ref, wcat_ref, bmat_ref, pw_ref, ps_ref,
                   ya_ref, yc_ref, ext_ref, *, seq_len):
    i = pl.program_id(0)
    tm, d_a2 = z_ref.shape
    d_a = d_a2 // 2
    hd = d_a // H_A
    d_c = c_ref.shape[1]
    gd = d_c // G_C

    z = z_ref[...]
    g = 0.5 * z * (1.0 + lax.erf(z * (1.0 / math.sqrt(2.0))))
    u = g[:, :d_a]
    v = g[:, d_a:]
    ri = lax.broadcasted_iota(jnp.int32, (d_a, d_a), 0) // hd
    ci = lax.broadcasted_iota(jnp.int32, (d_a, d_a), 1) // hd
    grp = jnp.where(ri == ci, 1.0, 0.0).astype(BF16)

    def group_mean(a):
        hi, lo = _split_bf16(a)
        return (_dot(hi, grp) + _dot(lo, grp)) * (1.0 / hd)

    mu = group_mean(v)
    dv = v - mu
    var = group_mean(dv * dv)
    vn = dv * lax.rsqrt(var + EPS)

    wr = lax.broadcasted_iota(jnp.int32, wcat_ref.shape, 0)
    wc = lax.broadcasted_iota(jnp.int32, wcat_ref.shape, 1) % CHUNK
    wcat = jnp.where(wc <= wr, wcat_ref[...], jnp.zeros_like(wcat_ref[...]))
    lane_head = lax.broadcasted_iota(jnp.int32, (CHUNK, d_a), 1) // hd
    bmat = bmat_ref[...]
    for ck in range(tm // CHUNK):
        rows = slice(ck * CHUNK, (ck + 1) * CHUNK)
        vn_c = vn[rows]
        stacked = jnp.concatenate(
            [jnp.where(lane_head == h, vn_c, 0.0).astype(BF16) for h in range(H_A)], axis=0)
        sg = _dot(wcat, stacked) + bmat
        ya_ref[rows, :] = (u[rows] * sg).astype(ya_ref.dtype)

    c = c_ref[...]
    pos0 = (i * tm) % seq_len
    prev = jnp.where(pos0 == 0, jnp.zeros_like(cprev_ref[...]), cprev_ref[...])
    ext_ref[0:POOL_HALO, :] = prev
    ext_ref[POOL_HALO:, :] = c
    lane_grp = lax.broadcasted_iota(jnp.int32, (tm, d_c), 1) // gd
    run = c
    wsum = jnp.zeros_like(c)
    win = jnp.zeros((tm, d_c), jnp.int32)
    k = 1
    for gi, w in enumerate(POOL_WINDOWS):
        while k < w:
            run = run + ext_ref[pl.ds(POOL_HALO - k, tm), :]
            k += 1
        wsum = jnp.where(lane_grp == gi, run, wsum)
        win = jnp.where(lane_grp == gi, w, win)
    pos = pos0 + lax.broadcasted_iota(jnp.int32, (tm, d_c), 0)
    cnt = jnp.minimum(pos + 1, win).astype(F32)
    y = (wsum / cnt - c).astype(BF16)
    yc_ref[...] = (_dot(y, pw_ref[...]) * ps_ref[...]).astype(yc_ref.dtype)


def _mix_ac(pf, wcat, bmat, pw, ps, seq_len, d_a, d_c):
    t = pf.shape[0]
    tm = ROW_TILE
    halo_blocks = tm // POOL_HALO
    c_col = (2 * d_a) // d_c
    return pl.pallas_call(
        functools.partial(_mix_ac_kernel, seq_len=seq_len),
        out_shape=(jax.ShapeDtypeStruct((t, d_a), BF16), jax.ShapeDtypeStruct((t, d_c), BF16)),
        grid=(t // tm,),
        in_specs=[pl.BlockSpec((tm, 2 * d_a), lambda i: (i, 0)),
                  pl.BlockSpec((tm, d_c), lambda i: (i, c_col)),
                  pl.BlockSpec((POOL_HALO, d_c), lambda i: (jnp.maximum(i * halo_blocks - 1, 0), c_col)),
                  pl.BlockSpec(wcat.shape, lambda i: (0, 0)),
                  pl.BlockSpec(bmat.shape, lambda i: (0, 0)),
                  pl.BlockSpec(pw.shape, lambda i: (0, 0)),
                  pl.BlockSpec(ps.shape, lambda i: (0, 0))],
        out_specs=(pl.BlockSpec((tm, d_a), lambda i: (i, 0)),
                   pl.BlockSpec((tm, d_c), lambda i: (i, 0))),
        scratch_shapes=[pltpu.VMEM((POOL_HALO + tm, d_c), F32)],
        compiler_params=_cparams(("arbitrary",), 48),
        name="mix_ac",
    )(pf, pf, pf, wcat, bmat, pw, ps)


def _attn_kernel(lam_ref, q_ref, k_ref, v_ref, bias_ref, g_ref, o_ref,
                 qs_ref, m_ref, l_ref, acc_ref, s0_ref, s1_ref, c0_ref, c1_ref):
    i = pl.program_id(2)
    tb, dv = q_ref.shape

    q = q_ref[...]
    lane = lax.broadcasted_iota(jnp.int32, (tb, dv), 1)
    qs_ref[0:tb, :] = jnp.where(lane < DH_B, q, jnp.zeros_like(q))
    qs_ref[tb:, :] = jnp.where(lane >= DH_B, q, jnp.zeros_like(q))
    m_ref[...] = jnp.full(m_ref.shape, -jnp.inf, F32)
    l_ref[...] = jnp.zeros(l_ref.shape, F32)
    acc_ref[...] = jnp.zeros(acc_ref.shape, F32)

    bufs = ((s0_ref, c0_ref), (s1_ref, c1_ref))

    def scores(buf, j, bias=None):
        s_ref, cm_ref = buf
        kj = k_ref[pl.ds(pl.multiple_of(j * tb, tb), tb), :]
        s = lax.dot_general(kj, qs_ref[...], (((1,), (1,)), ((), ())), preferred_element_type=F32)
        if bias is not None:
            s = s + jnp.concatenate([bias, bias], axis=1)
        s_ref[...] = s
        cm_ref[...] = jnp.max(s, axis=0, keepdims=True)

    def consume(buf, j, bias=None):
        s_ref, cm_ref = buf
        vjt = jnp.transpose(v_ref[pl.ds(pl.multiple_of(j * tb, tb), tb), :])
        s = s_ref[...]
        cm = cm_ref[...]
        if bias is not None:
            s = s + jnp.concatenate([bias, bias], axis=1)
            cm = jnp.max(s, axis=0, keepdims=True)
        m_prev = m_ref[...]
        m_next = jnp.maximum(m_prev, cm)
        alpha = jnp.exp2(m_prev - m_next)
        p = jnp.exp2(s - m_next)
        l_ref[...] = alpha * l_ref[...] + jnp.sum(p, axis=0, keepdims=True)
        m_ref[...] = m_next
        acc_ref[...] = alpha * acc_ref[...] + _dot(vjt, p.astype(BF16))

    def pipelined(base, n):
        for u in range(n):
            scores(bufs[(u + 1) % 2], base + u + 1)
            consume(bufs[u % 2], base + u)

    n_far = jnp.maximum(i - 1, 0)

    @pl.when(i == 0)
    def _():
        scores(bufs[0], 0, bias_ref[1])
        consume(bufs[0], 0)

    @pl.when(i == 1)
    def _():
        scores(bufs[0], 0, bias_ref[0])
        scores(bufs[1], 1, bias_ref[1])
        consume(bufs[0], 0)
        consume(bufs[1], 1)

    @pl.when(i >= 2)
    def _():
        scores(bufs[0], 0)

        def quad_body(t, carry):
            pipelined(4 * t, 4)
            return carry

        lax.fori_loop(0, n_far // 4, quad_body, 0)

        def pair_body(t, carry):
            pipelined((n_far // 4) * 4 + 2 * t, 2)
            return carry

        lax.fori_loop(0, (n_far % 4) // 2, pair_body, 0)

        @pl.when(n_far % 2 == 0)
        def _():
            scores(bufs[1], i, bias_ref[1])
            consume(bufs[0], i - 1, bias_ref[0])
            consume(bufs[1], i)

        @pl.when(n_far % 2 == 1)
        def _():
            scores(bufs[1], i - 1, bias_ref[0])
            consume(bufs[0], i - 2)
            scores(bufs[0], i, bias_ref[1])
            consume(bufs[1], i - 1)
            consume(bufs[0], i)

    o = acc_ref[...] / l_ref[...]
    d = o[:, :tb] - lam_ref[0] * o[:, tb:]
    ms = jnp.mean(d * d, axis=0, keepdims=True)
    y = d * lax.rsqrt(ms + EPS) * g_ref[...]
    o_ref[...] = jnp.transpose(y).astype(o_ref.dtype)


def _attention(qkv, bias, lam, gain, batch, seq_len):
    t = qkv.shape[0]
    tb = ATTN_BLOCK
    nq = seq_len // tb
    dv = 2 * DH_B
    return pl.pallas_call(
        _attn_kernel,
        out_shape=jax.ShapeDtypeStruct((t, H_B * dv), BF16),
        grid=(batch, H_B, nq),
        in_specs=[pl.BlockSpec(memory_space=pltpu.SMEM),
                  pl.BlockSpec((tb, dv), lambda b, h, i: (b * nq + i, h)),
                  pl.BlockSpec((seq_len, dv), lambda b, h, i: (b, H_B + h)),
                  pl.BlockSpec((seq_len, dv), lambda b, h, i: (b, 2 * H_B + h)),
                  pl.BlockSpec((None, 2, tb, tb), lambda b, h, i: (h, 0, 0, 0)),
                  pl.BlockSpec((dv, 1), lambda b, h, i: (0, 0))],
        out_specs=pl.BlockSpec((tb, dv), lambda b, h, i: (b * nq + i, h)),
        scratch_shapes=[pltpu.VMEM((2 * tb, dv), BF16),
                        pltpu.VMEM((1, 2 * tb), F32),
                        pltpu.VMEM((1, 2 * tb), F32),
                        pltpu.VMEM((dv, 2 * tb), F32),
                        pltpu.VMEM((tb, 2 * tb), F32),
                        pltpu.VMEM((tb, 2 * tb), F32),
                        pltpu.VMEM((1, 2 * tb), F32),
                        pltpu.VMEM((1, 2 * tb), F32)],
        compiler_params=_cparams(("arbitrary", "arbitrary", "arbitrary"), 56),
        name="diff_attn",
    )(lam, qkv, qkv, qkv, bias, gain)


def _bias_kernel(rel_ref, o_ref):
    h = pl.program_id(0)
    tb = o_ref.shape[-1]
    kk = lax.broadcasted_iota(jnp.int32, (tb, tb), 0)
    qq = lax.broadcasted_iota(jnp.int32, (tb, tb), 1)
    far = rel_ref[(NUM_BUCKETS - 1) * H_B + h]
    for w, off in enumerate((tb, 0)):
        dist = off + qq - kk
        n = jnp.maximum(dist, 0)
        nf = jnp.maximum(n, 1).astype(F32)
        large = MAX_EXACT + (jnp.log(nf / MAX_EXACT) / math.log(MAX_DISTANCE / MAX_EXACT)
                             * (NUM_BUCKETS - MAX_EXACT)).astype(jnp.int32)
        bucket = jnp.where(n < MAX_EXACT, n, jnp.minimum(large, NUM_BUCKETS - 1))
        b = jnp.zeros((tb, tb), F32)
        for k in range(NUM_BUCKETS):
            b = jnp.where(bucket == k, rel_ref[k * H_B + h], b)
        o_ref[w] = jnp.where(dist >= 0, (b - far) * math.log2(math.e), -jnp.inf)


def _attn_bias_tiles(rel_bias, tb):
    return pl.pallas_call(
        _bias_kernel,
        out_shape=jax.ShapeDtypeStruct((H_B, 2, tb, tb), F32),
        grid=(H_B,),
        in_specs=[pl.BlockSpec(memory_space=pltpu.SMEM)],
        out_specs=pl.BlockSpec((None, 2, tb, tb), lambda h: (h, 0, 0, 0)),
        compiler_params=_cparams(("arbitrary",), 48),
        name="attn_bias",
    )(rel_bias.reshape(NUM_BUCKETS * H_B))


def _out_proj_kernel(x_ref, ya_ref, yb_ref, yc_ref, w_ref, g_ref, b_ref, o_ref, *, alpha):
    d_a = ya_ref.shape[1]
    d_b = yb_ref.shape[1]
    h = (_dot(ya_ref[...], w_ref[0:d_a, :]) + _dot(yb_ref[...], w_ref[d_a:d_a + d_b, :])
         + _dot(yc_ref[...], w_ref[d_a + d_b:, :]))
    o_ref[...] = _layer_norm(alpha * x_ref[...] + h, g_ref[...], b_ref[...])


def _out_proj(x, ya, yb, yc, w, g, b, alpha):
    t, d = x.shape
    tm = ROW_TILE
    row = lambda n: pl.BlockSpec((tm, n), lambda i: (i, 0))
    full = lambda a: pl.BlockSpec(a.shape, lambda i: (0, 0))
    return pl.pallas_call(
        functools.partial(_out_proj_kernel, alpha=alpha),
        out_shape=jax.ShapeDtypeStruct((t, d), F32),
        grid=(t // tm,),
        in_specs=[row(d), row(ya.shape[1]), row(yb.shape[1]), row(yc.shape[1]), full(w), full(g), full(b)],
        out_specs=row(d),
        compiler_params=_cparams(("arbitrary",), 48),
        name="out_proj_ln",
    )(x, ya, yb, yc, w, g, b)


def _swiglu_chunk(xb, w1, w3, w2):
    h1 = _dot(xb, w1)
    h3 = _dot(xb, w3)
    gate = h1 * (1.0 / (1.0 + jnp.exp(-h1)))
    return _dot((gate * h3).astype(BF16), w2)


def _ffn_kernel(x_ref, w1_ref, w3_ref, w2_ref, g_ref, b_ref, o_ref, xb_ref, acc_ref, *, alpha):
    c = pl.program_id(1)

    @pl.when(c == 0)
    def _():
        xb_ref[...] = x_ref[...].astype(BF16)
        acc_ref[...] = jnp.zeros(acc_ref.shape, F32)

    acc_ref[...] += _swiglu_chunk(xb_ref[...], w1_ref[...], w3_ref[...], w2_ref[...])

    @pl.when(c == pl.num_programs(1) - 1)
    def _():
        o_ref[...] = _layer_norm(alpha * x_ref[...] + acc_ref[...], g_ref[...], b_ref[...])


def _ffn(x, w1, w3, w2, g, b, alpha, n_chunks):
    t, d = x.shape
    f = w1.shape[1]
    fc = f // n_chunks
    tm = ROW_TILE
    return pl.pallas_call(
        functools.partial(_ffn_kernel, alpha=alpha),
        out_shape=jax.ShapeDtypeStruct((t, d), F32),
        grid=(t // tm, n_chunks),
        in_specs=[pl.BlockSpec((tm, d), lambda i, c: (i, 0)),
                  pl.BlockSpec((d, fc), lambda i, c: (0, c)),
                  pl.BlockSpec((d, fc), lambda i, c: (0, c)),
                  pl.BlockSpec((fc, d), lambda i, c: (c, 0)),
                  pl.BlockSpec(g.shape, lambda i, c: (0, 0)),
                  pl.BlockSpec(b.shape, lambda i, c: (0, 0))],
        out_specs=pl.BlockSpec((tm, d), lambda i, c: (i, 0)),
        scratch_shapes=[pltpu.VMEM((tm, d), BF16), pltpu.VMEM((tm, d), F32)],
        compiler_params=_cparams(("arbitrary", "arbitrary"), 56),
        name="ffn_ln",
    )(x, w1, w3, w2, g, b)


def _router_kernel(x_ref, wh_ref, wl_ref, r_ref, cnt_ref, carry_ref):
    i = pl.program_id(0)
    tr = x_ref.shape[0]
    ne = N_EXPERTS

    @pl.when(i == 0)
    def _():
        carry_ref[...] = jnp.zeros(carry_ref.shape, F32)

    xh, xl = _split_bf16(x_ref[...])
    logits = _dot(xh, wh_ref[...]) + _dot(xl, wh_ref[...]) + _dot(xh, wl_ref[...])
    lt = jnp.transpose(logits)[0:ne, :]

    eio = lax.broadcasted_iota(jnp.int32, (ne, tr), 0)
    v1 = jnp.max(lt, axis=0, keepdims=True)
    i1 = jnp.min(jnp.where(lt == v1, eio, ne), axis=0, keepdims=True)
    oh1 = eio == i1
    lt2 = jnp.where(oh1, -jnp.inf, lt)
    v2 = jnp.max(lt2, axis=0, keepdims=True)
    i2 = jnp.min(jnp.where(lt2 == v2, eio, ne), axis=0, keepdims=True)
    oh2 = eio == i2
    e = jnp.exp(v2 - v1)
    g1 = 1.0 / (1.0 + e)
    g2 = e / (1.0 + e)

    member = jnp.where(oh1, 1.0, jnp.where(oh2, 1.0, 0.0))
    tri_r = lax.broadcasted_iota(jnp.int32, (tr, tr), 0)
    tri_c = lax.broadcasted_iota(jnp.int32, (tr, tr), 1)
    upper = jnp.where(tri_r < tri_c, 1.0, 0.0).astype(BF16)
    ranks = _dot(member.astype(BF16), upper) + carry_ref[:, 0:1]
    r1 = jnp.sum(jnp.where(oh1, ranks, 0.0), axis=0, keepdims=True)
    r2 = jnp.sum(jnp.where(oh2, ranks, 0.0), axis=0, keepdims=True)
    carry_ref[...] = carry_ref[...] + jnp.sum(member, axis=1, keepdims=True)
    cnt_ref[...] = carry_ref[...]

    rows = [i1.astype(F32), i2.astype(F32), g1, g2, r1, r2]
    out = jnp.zeros((ne, tr), F32)
    for k, row in enumerate(rows):
        out = jnp.where(eio == k, row, out)
    r_ref[...] = out


def _router(x, wh, wl):
    t, d = x.shape
    tr = ROUTER_TILE
    return pl.pallas_call(
        _router_kernel,
        out_shape=(jax.ShapeDtypeStruct((N_EXPERTS, t), F32), jax.ShapeDtypeStruct((N_EXPERTS, LANES), F32)),
        grid=(t // tr,),
        in_specs=[pl.BlockSpec((tr, d), lambda i: (i, 0)),
                  pl.BlockSpec(wh.shape, lambda i: (0, 0)),
                  pl.BlockSpec(wl.shape, lambda i: (0, 0))],
        out_specs=(pl.BlockSpec((N_EXPERTS, tr), lambda i: (0, i)),
                   pl.BlockSpec((N_EXPERTS, LANES), lambda i: (0, 0))),
        scratch_shapes=[pltpu.VMEM((N_EXPERTS, LANES), F32)],
        compiler_params=_cparams(("arbitrary",), 48),
        name="router",
    )(x, wh, wl)


def _gather_rows(idx_vmem, idx_smem, idx_sem, src_hbm, dst, row_sem):
    cp = pltpu.make_async_copy(idx_vmem, idx_smem, idx_sem)
    cp.start()
    cp.wait()
    n_lists, n_groups = dst.shape[0], dst.shape[1]

    def body(g, carry):
        base = pl.multiple_of(g * SUBLANES, SUBLANES)
        for k in range(n_lists):
            for u in range(SUBLANES):
                t = idx_smem[k, base + u]
                pltpu.make_async_copy(src_hbm.at[pl.ds(t, 1)], dst.at[k, g, pl.ds(u, 1)], row_sem).start(
                    priority=u % 2)
        return carry

    lax.fori_loop(0, n_groups, body, 0)


def _moe_kernel(te_ref, na_ref, src_ref, srcn_ref, x_hbm, w1_ref, w3_ref, w2_ref, ys_ref,
                xbuf, xb_ref, acc_ref, idx_ref, row_sem, idx_sem):
    i = pl.program_id(0)
    c = pl.program_id(1)
    tm = xb_ref.shape[0]
    n_active = na_ref[0]
    slot = i % 2

    @pl.when(c == 0)
    def _():
        @pl.when(i == 0)
        def _():
            _gather_rows(src_ref.at[0], idx_ref, idx_sem, x_hbm, xbuf.at[0], row_sem.at[0])

        @pl.when(i < n_active)
        def _():
            pltpu.make_async_copy(xbuf.at[slot], xbuf.at[slot], row_sem.at[slot]).wait()
            xb_ref[...] = xbuf[slot, 0].reshape(xb_ref.shape).astype(BF16)
            acc_ref[...] = jnp.zeros(acc_ref.shape, F32)

        @pl.when(i + 1 < n_active)
        def _():
            _gather_rows(srcn_ref.at[0], idx_ref, idx_sem, x_hbm, xbuf.at[1 - slot], row_sem.at[1 - slot])

    @pl.when(i < n_active)
    def _():
        acc_ref[...] += _swiglu_chunk(xb_ref[...], w1_ref[...], w3_ref[...], w2_ref[...])

    @pl.when(c == pl.num_programs(1) - 1)
    def _():
        ys_ref[...] = jnp.where(i < n_active, acc_ref[...], 0.0)


def _moe_experts(tile_expert, n_active, src, x, w1, w3, w2, n_chunks):
    n_tiles, _, tm = src.shape
    d = x.shape[1]
    f = w1.shape[2]
    fc = f // n_chunks

    def chunk(i, c, na):
        return jnp.where(i < na[0], c, n_chunks - 1)

    grid_spec = pltpu.PrefetchScalarGridSpec(
        num_scalar_prefetch=2,
        grid=(n_tiles, n_chunks),
        in_specs=[pl.BlockSpec((1, 1, tm), lambda i, c, te, na: (i, 0, 0)),
                  pl.BlockSpec((1, 1, tm), lambda i, c, te, na: (jnp.minimum(i + 1, n_tiles - 1), 0, 0)),
                  pl.BlockSpec(memory_space=pl.ANY),
                  pl.BlockSpec((None, d, fc), lambda i, c, te, na: (te[i], 0, chunk(i, c, na))),
                  pl.BlockSpec((None, d, fc), lambda i, c, te, na: (te[i], 0, chunk(i, c, na))),
                  pl.BlockSpec((None, fc, d), lambda i, c, te, na: (te[i], chunk(i, c, na), 0))],
        out_specs=pl.BlockSpec((tm, d), lambda i, c, te, na: (i, 0)),
        scratch_shapes=[pltpu.VMEM((2, 1, tm // SUBLANES, SUBLANES, d), F32),
                        pltpu.VMEM((tm, d), BF16),
                        pltpu.VMEM((tm, d), F32),
                        pltpu.SMEM((1, tm), jnp.int32),
                        pltpu.SemaphoreType.DMA((2,)),
                        pltpu.SemaphoreType.DMA(())],
    )
    return pl.pallas_call(
        _moe_kernel,
        out_shape=jax.ShapeDtypeStruct((n_tiles * tm, d), F32),
        grid_spec=grid_spec,
        compiler_params=_cparams(("arbitrary", "arbitrary"), 56),
        name="moe_experts",
    )(tile_expert, n_active, src, src, x, w1, w3, w2)


def _combine_kernel(pos_ref, posn_ref, gate_ref, x_ref, ys_hbm, g_ref, b_ref, o_ref,
                    ybuf, idx_ref, row_sem, idx_sem, *, alpha):
    i = pl.program_id(0)
    tc = x_ref.shape[0]
    slot = i % 2

    @pl.when(i == 0)
    def _():
        _gather_rows(pos_ref.at[0], idx_ref, idx_sem, ys_hbm, ybuf.at[0], row_sem.at[0])

    pltpu.make_async_copy(ybuf.at[slot], ybuf.at[slot], row_sem.at[slot]).wait()

    @pl.when(i + 1 < pl.num_programs(0))
    def _():
        _gather_rows(posn_ref.at[0], idx_ref, idx_sem, ys_hbm, ybuf.at[1 - slot], row_sem.at[1 - slot])

    gates = gate_ref[...]
    f = (gates[:, 0:1] * ybuf[slot, 0].reshape(x_ref.shape)
         + gates[:, 1:2] * ybuf[slot, 1].reshape(x_ref.shape))
    o_ref[...] = _layer_norm(alpha * x_ref[...] + f, g_ref[...], b_ref[...])


def _combine(pos, gates, x, ys, g, b, alpha):
    t, d = x.shape
    tc = COMBINE_TILE
    n = t // tc
    return pl.pallas_call(
        functools.partial(_combine_kernel, alpha=alpha),
        out_shape=jax.ShapeDtypeStruct((t, d), F32),
        grid=(n,),
        in_specs=[pl.BlockSpec((1, 2, tc), lambda i: (i, 0, 0)),
                  pl.BlockSpec((1, 2, tc), lambda i: (jnp.minimum(i + 1, n - 1), 0, 0)),
                  pl.BlockSpec((tc, 2), lambda i: (i, 0)),
                  pl.BlockSpec((tc, d), lambda i: (i, 0)),
                  pl.BlockSpec(memory_space=pl.ANY),
                  pl.BlockSpec(g.shape, lambda i: (0, 0)),
                  pl.BlockSpec(b.shape, lambda i: (0, 0))],
        out_specs=pl.BlockSpec((tc, d), lambda i: (i, 0)),
        scratch_shapes=[pltpu.VMEM((2, 2, tc // SUBLANES, SUBLANES, d), F32),
                        pltpu.SMEM((2, tc), jnp.int32),
                        pltpu.SemaphoreType.DMA((2,)),
                        pltpu.SemaphoreType.DMA(())],
        compiler_params=_cparams(("arbitrary",), 48),
        name="moe_combine_ln",
    )(pos, pos, gates, x, ys, g, b)


def _moe_ffn(x, router_w, w1, w3, w2, g, b, alpha):
    t, d = x.shape
    tm = MOE_TILE
    rw = jnp.zeros((d, LANES), F32).at[:, :N_EXPERTS].set(router_w)
    wh = rw.astype(BF16)
    wl = (rw - wh.astype(F32)).astype(BF16)
    routed, counts = _router(x, wh, wl)

    e1 = routed[0].astype(jnp.int32)
    e2 = routed[1].astype(jnp.int32)
    r1 = routed[4].astype(jnp.int32)
    r2 = routed[5].astype(jnp.int32)
    cnt = counts[:, 0].astype(jnp.int32)
    tiles_per_expert = (cnt + tm - 1) // tm
    tile_end = jnp.cumsum(tiles_per_expert)
    row_start = (tile_end - tiles_per_expert) * tm
    pos1 = row_start[e1] + r1
    pos2 = row_start[e2] + r2
    n_tiles = (2 * t) // tm + N_EXPERTS
    n_active = tile_end[-1:]
    tile_expert = jnp.minimum(
        jnp.searchsorted(tile_end, jnp.minimum(jnp.arange(n_tiles), n_active[0] - 1), side="right"),
        N_EXPERTS - 1).astype(jnp.int32)
    tok = jnp.arange(t, dtype=jnp.int32)
    src = jnp.zeros((n_tiles * tm,), jnp.int32).at[jnp.concatenate([pos1, pos2])].set(
        jnp.concatenate([tok, tok]), unique_indices=True)

    ys = _moe_experts(tile_expert, n_active.astype(jnp.int32), src.reshape(n_tiles, 1, tm), x, w1, w3, w2, 4)

    tc = COMBINE_TILE
    pos = jnp.stack([pos1.reshape(t // tc, tc), pos2.reshape(t // tc, tc)], axis=1)
    gates = jnp.stack([routed[2], routed[3]], axis=1)
    return _combine(pos, gates, x, ys, g, b, alpha)


def kernel(x, w_in, w_out, gmlp_ws, gmlp_bs, lam_q1, lam_k1, lam_q2, lam_k2, diff_subln_g, rel_bias,
           pool_w, pool_scale, ln1_g, ln1_b, ln2_g, ln2_b, ffn_w1, ffn_w3, ffn_w2, router_w,
           moe_w1, moe_w3, moe_w2):
    batch, seq_len, d_model = x.shape
    depth = w_in.shape[0]
    d_a = w_out.shape[1] // 4
    d_b = H_B * 2 * DH_B
    d_c = pool_w.shape[1] * pool_w.shape[2]
    alpha = (2 * depth) ** 0.25
    t = batch * seq_len
    o_a = 2 * d_a
    o_q = o_a + d_b
    o_k = o_q + d_b
    o_v = o_k + d_b

    xt = x.reshape(t, d_model)
    bias = _attn_bias_tiles(rel_bias, ATTN_BLOCK)
    q_scale = DH_B ** -0.5 * math.log2(math.e)

    for l in range(depth):
        wl = w_in[l]
        w_cat = jnp.concatenate([wl[:, :o_a], wl[:, o_v:], wl[:, o_a:o_q] * q_scale, wl[:, o_q:o_v]],
                                axis=1).astype(BF16)
        wcat = jnp.transpose(gmlp_ws[l], (1, 0, 2)).reshape(CHUNK, H_A * CHUNK).astype(BF16)
        bmat = jnp.repeat(gmlp_bs[l].T, d_a // H_A, axis=1)
        pw = jax.scipy.linalg.block_diag(*[pool_w[l, gi] for gi in range(G_C)]).astype(BF16)
        ps = pool_scale[l].reshape(1, d_c)
        lam_init = 0.8 - 0.6 * math.exp(-0.3 * l)
        lam = (jnp.exp(jnp.sum(lam_q1[l] * lam_k1[l])) - jnp.exp(jnp.sum(lam_q2[l] * lam_k2[l]))
               + lam_init).reshape(1).astype(F32)
        gain = (diff_subln_g[l] * (1.0 - lam_init)).reshape(2 * DH_B, 1)

        pf, qkv = _in_proj(xt, w_cat, o_a + d_c)
        ya, yc = _mix_ac(pf, wcat, bmat, pw, ps, seq_len, d_a, d_c)
        yb = _attention(qkv, bias, lam, gain, batch, seq_len)
        xt = _out_proj(xt, ya, yb, yc, w_out[l].astype(BF16), ln1_g[l].reshape(1, -1), ln1_b[l].reshape(1, -1),
                       alpha)
        g2 = ln2_g[l].reshape(1, -1)
        b2 = ln2_b[l].reshape(1, -1)
        if l % 2 == 0:
            i = l // 2
            xt = _ffn(xt, ffn_w1[i].astype(BF16), ffn_w3[i].astype(BF16), ffn_w2[i].astype(BF16), g2, b2,
                      alpha, 2)
        else:
            i = l // 2
            xt = _moe_ffn(xt, router_w[i], moe_w1[i].astype(BF16), moe_w3[i].astype(BF16),
                          moe_w2[i].astype(BF16), g2, b2, alpha)
    return xt.reshape(batch, seq_len, d_model)
```

```python
import functools
import math

import jax
import jax.numpy as jnp
from jax import lax
from jax.experimental import pallas as pl
from jax.experimental.pallas import tpu as pltpu

F32 = jnp.float32
BF16 = jnp.bfloat16

H_A = 4
CHUNK = 128
H_B = 4
DH_B = 64
G_C = 4
POOL_WINDOWS = (2, 4, 8, 16)
POOL_HALO = 16
NUM_BUCKETS = 32
MAX_EXACT = NUM_BUCKETS // 2
MAX_DISTANCE = 128
N_EXPERTS = 8
EPS = 1e-5

LANES = 128
SUBLANES = 8
MIB = 1024 * 1024

ROW_TILE = 512
ATTN_BLOCK = 512
MOE_TILE = 1024
COMBINE_TILE = 256


def _cparams(semantics, vmem_mib):
    return pltpu.CompilerParams(dimension_semantics=semantics, vmem_limit_bytes=vmem_mib * MIB)


def _dot(a, b):
    return jnp.dot(a, b, preferred_element_type=F32)


def _layer_norm(r, g, b):
    mu = jnp.mean(r, axis=-1, keepdims=True)
    d = r - mu
    var = jnp.mean(d * d, axis=-1, keepdims=True)
    return d * lax.rsqrt(var + EPS) * g + b


def _split_bf16(a):
    hi = a.astype(BF16)
    lo = (a - hi.astype(F32)).astype(BF16)
    return hi, lo


def _in_mix_kernel(x_ref, w_ref, wcat_ref, bmat_ref, pw_ref, ps_ref,
                   qkv_ref, ya_ref, yc_ref, ext_ref, halo_ref, *, seq_len):
    i = pl.program_id(0)
    tm = x_ref.shape[0]
    d_a = ya_ref.shape[1]
    d_c = yc_ref.shape[1]
    hd = d_a // H_A
    gd = d_c // G_C
    n_mix = 2 * d_a + d_c

    xb = x_ref[...].astype(BF16)
    pf = _dot(xb, w_ref[:, :n_mix])
    qkv_ref[...] = _dot(xb, w_ref[:, n_mix:]).astype(BF16)

    z = pf[:, :2 * d_a]
    g = 0.5 * z * (1.0 + lax.erf(z * (1.0 / math.sqrt(2.0))))
    u = g[:, :d_a]
    v = g[:, d_a:]
    ri = lax.broadcasted_iota(jnp.int32, (d_a, d_a), 0) // hd
    ci = lax.broadcasted_iota(jnp.int32, (d_a, d_a), 1) // hd
    grp = jnp.where(ri == ci, 1.0, 0.0).astype(BF16)

    def group_mean(a):
        hi, lo = _split_bf16(a)
        return (_dot(hi, grp) + _dot(lo, grp)) * (1.0 / hd)

    mu = group_mean(v)
    dv = v - mu
    var = group_mean(dv * dv)
    vn = dv * lax.rsqrt(var + EPS)

    wr = lax.broadcasted_iota(jnp.int32, wcat_ref.shape, 0)
    wc = lax.broadcasted_iota(jnp.int32, wcat_ref.shape, 1) % CHUNK
    wcat = jnp.where(wc <= wr, wcat_ref[...], jnp.zeros_like(wcat_ref[...]))
    lane_head = lax.broadcasted_iota(jnp.int32, (CHUNK, d_a), 1) // hd
    bmat = bmat_ref[...]
    for ck in range(tm // CHUNK):
        rows = slice(ck * CHUNK, (ck + 1) * CHUNK)
        vn_c = vn[rows]
        stacked = jnp.concatenate(
            [jnp.where(lane_head == h, vn_c, 0.0).astype(BF16) for h in range(H_A)], axis=0)
        sg = _dot(wcat, stacked) + bmat
        ya_ref[rows, :] = (u[rows] * sg).astype(ya_ref.dtype)

    c = pf[:, 2 * d_a:]
    pos0 = (i * tm) % seq_len

    @pl.when(pos0 == 0)
    def _():
        halo_ref[...] = jnp.zeros(halo_ref.shape, F32)

    ext_ref[0:POOL_HALO, :] = halo_ref[...]
    ext_ref[POOL_HALO:, :] = c
    halo_ref[...] = c[tm - POOL_HALO:, :]
    lane_grp = lax.broadcasted_iota(jnp.int32, (tm, d_c), 1) // gd
    run = c
    wsum = jnp.zeros_like(c)
    win = jnp.zeros((tm, d_c), jnp.int32)
    k = 1
    for gi, w in enumerate(POOL_WINDOWS):
        while k < w:
            run = run + ext_ref[pl.ds(POOL_HALO - k, tm), :]
            k += 1
        wsum = jnp.where(lane_grp == gi, run, wsum)
        win = jnp.where(lane_grp == gi, w, win)
    pos = pos0 + lax.broadcasted_iota(jnp.int32, (tm, d_c), 0)
    cnt = jnp.minimum(pos + 1, win).astype(F32)
    y = (wsum / cnt - c).astype(BF16)
    yc_ref[...] = (_dot(y, pw_ref[...]) * ps_ref[...]).astype(yc_ref.dtype)


def _in_mix(x, w, wcat, bmat, pw, ps, seq_len, d_a, d_c):
    t, d = x.shape
    n = w.shape[1]
    n_qkv = n - 2 * d_a - d_c
    tm = ROW_TILE
    full = lambda a: pl.BlockSpec(a.shape, lambda i: (0, 0))
    return pl.pallas_call(
        functools.partial(_in_mix_kernel, seq_len=seq_len),
        out_shape=(jax.ShapeDtypeStruct((t, n_qkv), BF16), jax.ShapeDtypeStruct((t, d_a), BF16),
                   jax.ShapeDtypeStruct((t, d_c), BF16)),
        grid=(t // tm,),
        in_specs=[pl.BlockSpec((tm, d), lambda i: (i, 0)), full(w), full(wcat), full(bmat), full(pw), full(ps)],
        out_specs=(pl.BlockSpec((tm, n_qkv), lambda i: (i, 0)),
                   pl.BlockSpec((tm, d_a), lambda i: (i, 0)),
                   pl.BlockSpec((tm, d_c), lambda i: (i, 0))),
        scratch_shapes=[pltpu.VMEM((POOL_HALO + tm, d_c), F32), pltpu.VMEM((POOL_HALO, d_c), F32)],
        compiler_params=_cparams(("arbitrary",), 48),
        name="in_proj_mix_ac",
    )(x, w, wcat, bmat, pw, ps)


def _attn_kernel(lam_ref, q_ref, k_ref, v_ref, bias_ref, g_ref, o_ref,
                 qs_ref, m_ref, l_ref, acc_ref, s0_ref, s1_ref, c0_ref, c1_ref):
    i = pl.program_id(2)
    tb, dv = q_ref.shape

    q = q_ref[...]
    lane = lax.broadcasted_iota(jnp.int32, (tb, dv), 1)
    qs_ref[0:tb, :] = jnp.where(lane < DH_B, q, jnp.zeros_like(q))
    qs_ref[tb:, :] = jnp.where(lane >= DH_B, q, jnp.zeros_like(q))
    m_ref[...] = jnp.full(m_ref.shape, -jnp.inf, F32)
    l_ref[...] = jnp.zeros(l_ref.shape, F32)
    acc_ref[...] = jnp.zeros(acc_ref.shape, F32)

    bufs = ((s0_ref, c0_ref), (s1_ref, c1_ref))

    def scores(buf, j, bias=None):
        s_ref, cm_ref = buf
        kj = k_ref[pl.ds(pl.multiple_of(j * tb, tb), tb), :]
        s = lax.dot_general(kj, qs_ref[...], (((1,), (1,)), ((), ())), preferred_element_type=F32)
        if bias is not None:
            s = s + jnp.concatenate([bias, bias], axis=1)
        s_ref[...] = s
        cm_ref[...] = jnp.max(s, axis=0, keepdims=True)

    def consume(buf, j, bias=None):
        s_ref, cm_ref = buf
        vjt = jnp.transpose(v_ref[pl.ds(pl.multiple_of(j * tb, tb), tb), :])
        s = s_ref[...]
        cm = cm_ref[...]
        if bias is not None:
            s = s + jnp.concatenate([bias, bias], axis=1)
            cm = jnp.max(s, axis=0, keepdims=True)
        m_prev = m_ref[...]
        m_next = jnp.maximum(m_prev, cm)
        alpha = jnp.exp2(m_prev - m_next)
        p = jnp.exp2(s - m_next)
        l_ref[...] = alpha * l_ref[...] + jnp.sum(p, axis=0, keepdims=True)
        m_ref[...] = m_next
        acc_ref[...] = alpha * acc_ref[...] + _dot(vjt, p.astype(BF16))

    def pipelined(base, n):
        for u in range(n):
            scores(bufs[(u + 1) % 2], base + u + 1)
            consume(bufs[u % 2], base + u)

    n_far = jnp.maximum(i - 1, 0)

    @pl.when(i == 0)
    def _():
        scores(bufs[0], 0, bias_ref[1])
        consume(bufs[0], 0)

    @pl.when(i == 1)
    def _():
        scores(bufs[0], 0, bias_ref[0])
        scores(bufs[1], 1, bias_ref[1])
        consume(bufs[0], 0)
        consume(bufs[1], 1)

    @pl.when(i >= 2)
    def _():
        scores(bufs[0], 0)

        def oct_body(t, carry):
            pipelined(8 * t, 8)
            return carry

        lax.fori_loop(0, n_far // 8, oct_body, 0)

        def quad_body(t, carry):
            pipelined((n_far // 8) * 8 + 4 * t, 4)
            return carry

        lax.fori_loop(0, (n_far % 8) // 4, quad_body, 0)

        def pair_body(t, carry):
            pipelined((n_far // 4) * 4 + 2 * t, 2)
            return carry

        lax.fori_loop(0, (n_far % 4) // 2, pair_body, 0)

        @pl.when(n_far % 2 == 0)
        def _():
            scores(bufs[1], i, bias_ref[1])
            consume(bufs[0], i - 1, bias_ref[0])
            consume(bufs[1], i)

        @pl.when(n_far % 2 == 1)
        def _():
            scores(bufs[1], i - 1, bias_ref[0])
            consume(bufs[0], i - 2)
            scores(bufs[0], i, bias_ref[1])
            consume(bufs[1], i - 1)
            consume(bufs[0], i)

    o = acc_ref[...] / l_ref[...]
    d = o[:, :tb] - lam_ref[0] * o[:, tb:]
    ms = jnp.mean(d * d, axis=0, keepdims=True)
    y = d * lax.rsqrt(ms + EPS) * g_ref[...]
    o_ref[...] = jnp.transpose(y).astype(o_ref.dtype)


def _attention(qkv, bias, lam, gain, batch, seq_len):
    t = qkv.shape[0]
    tb = ATTN_BLOCK
    nq = seq_len // tb
    dv = 2 * DH_B
    return pl.pallas_call(
        _attn_kernel,
        out_shape=jax.ShapeDtypeStruct((t, H_B * dv), BF16),
        grid=(batch, H_B, nq),
        in_specs=[pl.BlockSpec(memory_space=pltpu.SMEM),
                  pl.BlockSpec((tb, dv), lambda b, h, i: (b * nq + i, h)),
                  pl.BlockSpec((seq_len, dv), lambda b, h, i: (b, H_B + h)),
                  pl.BlockSpec((seq_len, dv), lambda b, h, i: (b, 2 * H_B + h)),
                  pl.BlockSpec((None, 2, tb, tb), lambda b, h, i: (h, 0, 0, 0)),
                  pl.BlockSpec((dv, 1), lambda b, h, i: (0, 0))],
        out_specs=pl.BlockSpec((tb, dv), lambda b, h, i: (b * nq + i, h)),
        scratch_shapes=[pltpu.VMEM((2 * tb, dv), BF16),
                        pltpu.VMEM((1, 2 * tb), F32),
                        pltpu.VMEM((1, 2 * tb), F32),
                        pltpu.VMEM((dv, 2 * tb), F32),
                        pltpu.VMEM((tb, 2 * tb), F32),
                        pltpu.VMEM((tb, 2 * tb), F32),
                        pltpu.VMEM((1, 2 * tb), F32),
                        pltpu.VMEM((1, 2 * tb), F32)],
        compiler_params=_cparams(("arbitrary", "arbitrary", "arbitrary"), 56),
        name="diff_attn",
    )(lam, qkv, qkv, qkv, bias, gain)


def _bias_kernel(rel_ref, o_ref):
    h = pl.program_id(0)
    tb = o_ref.shape[-1]
    kk = lax.broadcasted_iota(jnp.int32, (tb, tb), 0)
    qq = lax.broadcasted_iota(jnp.int32, (tb, tb), 1)
    far = rel_ref[(NUM_BUCKETS - 1) * H_B + h]
    for w, off in enumerate((tb, 0)):
        dist = off + qq - kk
        n = jnp.maximum(dist, 0)
        nf = jnp.maximum(n, 1).astype(F32)
        large = MAX_EXACT + (jnp.log(nf / MAX_EXACT) / math.log(MAX_DISTANCE / MAX_EXACT)
                             * (NUM_BUCKETS - MAX_EXACT)).astype(jnp.int32)
        bucket = jnp.where(n < MAX_EXACT, n, jnp.minimum(large, NUM_BUCKETS - 1))
        b = jnp.zeros((tb, tb), F32)
        for k in range(NUM_BUCKETS):
            b = jnp.where(bucket == k, rel_ref[k * H_B + h], b)
        o_ref[w] = jnp.where(dist >= 0, (b - far) * math.log2(math.e), -jnp.inf)


def _attn_bias_tiles(rel_bias, tb):
    return pl.pallas_call(
        _bias_kernel,
        out_shape=jax.ShapeDtypeStruct((H_B, 2, tb, tb), F32),
        grid=(H_B,),
        in_specs=[pl.BlockSpec(memory_space=pltpu.SMEM)],
        out_specs=pl.BlockSpec((None, 2, tb, tb), lambda h: (h, 0, 0, 0)),
        compiler_params=_cparams(("arbitrary",), 48),
        name="attn_bias",
    )(rel_bias.reshape(NUM_BUCKETS * H_B))


def _out_proj_kernel(x_ref, ya_ref, yb_ref, yc_ref, w_ref, g_ref, b_ref, o_ref, *, alpha):
    d_a = ya_ref.shape[1]
    d_b = yb_ref.shape[1]
    h = (_dot(ya_ref[...], w_ref[0:d_a, :]) + _dot(yb_ref[...], w_ref[d_a:d_a + d_b, :])
         + _dot(yc_ref[...], w_ref[d_a + d_b:, :]))
    o_ref[...] = _layer_norm(alpha * x_ref[...] + h, g_ref[...], b_ref[...])


def _out_proj(x, ya, yb, yc, w, g, b, alpha):
    t, d = x.shape
    tm = ROW_TILE
    row = lambda n: pl.BlockSpec((tm, n), lambda i: (i, 0))
    full = lambda a: pl.BlockSpec(a.shape, lambda i: (0, 0))
    return pl.pallas_call(
        functools.partial(_out_proj_kernel, alpha=alpha),
        out_shape=jax.ShapeDtypeStruct((t, d), F32),
        grid=(t // tm,),
        in_specs=[row(d), row(ya.shape[1]), row(yb.shape[1]), row(yc.shape[1]), full(w), full(g), full(b)],
        out_specs=row(d),
        compiler_params=_cparams(("arbitrary",), 48),
        name="out_proj_ln",
    )(x, ya, yb, yc, w, g, b)


def _swiglu_chunk(xb, w1, w3, w2):
    h1 = _dot(xb, w1)
    h3 = _dot(xb, w3)
    gate = h1 * (1.0 / (1.0 + jnp.exp(-h1)))
    return _dot((gate * h3).astype(BF16), w2)


def _ffn_kernel(x_ref, w1_ref, w3_ref, w2_ref, g_ref, b_ref, o_ref, xb_ref, acc_ref, *, alpha):
    c = pl.program_id(1)

    @pl.when(c == 0)
    def _():
        xb_ref[...] = x_ref[...].astype(BF16)
        acc_ref[...] = jnp.zeros(acc_ref.shape, F32)

    acc_ref[...] += _swiglu_chunk(xb_ref[...], w1_ref[...], w3_ref[...], w2_ref[...])

    @pl.when(c == pl.num_programs(1) - 1)
    def _():
        o_ref[...] = _layer_norm(alpha * x_ref[...] + acc_ref[...], g_ref[...], b_ref[...])


def _ffn(x, w1, w3, w2, g, b, alpha, n_chunks):
    t, d = x.shape
    f = w1.shape[1]
    fc = f // n_chunks
    tm = ROW_TILE
    return pl.pallas_call(
        functools.partial(_ffn_kernel, alpha=alpha),
        out_shape=jax.ShapeDtypeStruct((t, d), F32),
        grid=(t // tm, n_chunks),
        in_specs=[pl.BlockSpec((tm, d), lambda i, c: (i, 0)),
                  pl.BlockSpec((d, fc), lambda i, c: (0, c)),
                  pl.BlockSpec((d, fc), lambda i, c: (0, c)),
                  pl.BlockSpec((fc, d), lambda i, c: (c, 0)),
                  pl.BlockSpec(g.shape, lambda i, c: (0, 0)),
                  pl.BlockSpec(b.shape, lambda i, c: (0, 0))],
        out_specs=pl.BlockSpec((tm, d), lambda i, c: (i, 0)),
        scratch_shapes=[pltpu.VMEM((tm, d), BF16), pltpu.VMEM((tm, d), F32)],
        compiler_params=_cparams(("arbitrary", "arbitrary"), 56),
        name="ffn_ln",
    )(x, w1, w3, w2, g, b)


def _route_tile(x, wh_ref, wl_ref, r_ref, cnt_ref, carry_ref):
    i = pl.program_id(0)
    tr = x.shape[0]
    ne = N_EXPERTS

    @pl.when(i == 0)
    def _():
        carry_ref[...] = jnp.zeros(carry_ref.shape, F32)

    xh, xl = _split_bf16(x)
    logits = _dot(xh, wh_ref[...]) + _dot(xl, wh_ref[...]) + _dot(xh, wl_ref[...])
    lt = jnp.transpose(logits)[0:ne, :]

    eio = lax.broadcasted_iota(jnp.int32, (ne, tr), 0)
    v1 = jnp.max(lt, axis=0, keepdims=True)
    i1 = jnp.min(jnp.where(lt == v1, eio, ne), axis=0, keepdims=True)
    oh1 = eio == i1
    lt2 = jnp.where(oh1, -jnp.inf, lt)
    v2 = jnp.max(lt2, axis=0, keepdims=True)
    i2 = jnp.min(jnp.where(lt2 == v2, eio, ne), axis=0, keepdims=True)
    oh2 = eio == i2
    e = jnp.exp(v2 - v1)
    g1 = 1.0 / (1.0 + e)
    g2 = e / (1.0 + e)

    member = jnp.where(oh1, 1.0, jnp.where(oh2, 1.0, 0.0))
    tri_r = lax.broadcasted_iota(jnp.int32, (tr, tr), 0)
    tri_c = lax.broadcasted_iota(jnp.int32, (tr, tr), 1)
    upper = jnp.where(tri_r < tri_c, 1.0, 0.0).astype(BF16)
    ranks = _dot(member.astype(BF16), upper) + carry_ref[:, 0:1]
    r1 = jnp.sum(jnp.where(oh1, ranks, 0.0), axis=0, keepdims=True)
    r2 = jnp.sum(jnp.where(oh2, ranks, 0.0), axis=0, keepdims=True)
    carry_ref[...] = carry_ref[...] + jnp.sum(member, axis=1, keepdims=True)
    cnt_ref[...] = carry_ref[...]

    rows = [i1.astype(F32), i2.astype(F32), g1, g2, r1, r2]
    out = jnp.zeros((ne, tr), F32)
    for k, row in enumerate(rows):
        out = jnp.where(eio == k, row, out)
    r_ref[...] = out


def _router_kernel(x_ref, wh_ref, wl_ref, r_ref, cnt_ref, carry_ref):
    _route_tile(x_ref[...], wh_ref, wl_ref, r_ref, cnt_ref, carry_ref)


def _router(x, router_w):
    t, d = x.shape
    tr = ROW_TILE
    rw = jnp.zeros((d, LANES), F32).at[:, :N_EXPERTS].set(router_w)
    wh = rw.astype(BF16)
    wl = (rw - wh.astype(F32)).astype(BF16)
    return pl.pallas_call(
        _router_kernel,
        out_shape=(jax.ShapeDtypeStruct((N_EXPERTS, t), F32), jax.ShapeDtypeStruct((N_EXPERTS, LANES), F32)),
        grid=(t // tr,),
        in_specs=[pl.BlockSpec((tr, d), lambda i: (i, 0)),
                  pl.BlockSpec(wh.shape, lambda i: (0, 0)),
                  pl.BlockSpec(wl.shape, lambda i: (0, 0))],
        out_specs=(pl.BlockSpec((N_EXPERTS, tr), lambda i: (0, i)),
                   pl.BlockSpec((N_EXPERTS, LANES), lambda i: (0, 0))),
        scratch_shapes=[pltpu.VMEM((N_EXPERTS, LANES), F32)],
        compiler_params=_cparams(("arbitrary",), 48),
        name="router",
    )(x, wh, wl)


def _row_copies(idx_vmem, idx_smem, idx_sem, n_lists, n_groups, make_copy):
    cp = pltpu.make_async_copy(idx_vmem, idx_smem, idx_sem)
    cp.start()
    cp.wait()

    def body(g, carry):
        base = pl.multiple_of(g * SUBLANES, SUBLANES)
        for k in range(n_lists):
            for u in range(SUBLANES):
                make_copy(k, g, u, idx_smem[k, base + u]).start(priority=u % 2)
        return carry

    lax.fori_loop(0, n_groups, body, 0)


def _dispatch_kernel(zpos_ref, zlen_ref, pos_ref, x_ref, xs_hbm, xbuf, zbuf, idx_ref, row_sem, idx_sem, zero_sem):
    i = pl.program_id(0)
    n = pl.num_programs(0)
    td = x_ref.shape[0]
    slot = i % 2
    zrows = zbuf.shape[0]

    def zero_region(r, act):
        start, length = zpos_ref[r], zlen_ref[r]
        n_big = length // zrows
        rest = start + n_big * zrows

        def big(q, carry):
            act(pltpu.make_async_copy(
                zbuf, xs_hbm.at[pl.ds(pl.multiple_of(start + q * zrows, SUBLANES), zrows)], zero_sem))
            return carry

        def small(q, carry):
            act(pltpu.make_async_copy(
                zbuf.at[0:SUBLANES], xs_hbm.at[pl.ds(pl.multiple_of(rest + q * SUBLANES, SUBLANES), SUBLANES)],
                zero_sem))
            return carry

        lax.fori_loop(0, n_big, big, 0)
        lax.fori_loop(0, (length - n_big * zrows) // SUBLANES, small, 0)

    @pl.when(i == 0)
    def _():
        zbuf[...] = jnp.zeros(zbuf.shape, F32)
        for r in range(zpos_ref.shape[0]):
            zero_region(r, lambda cp: cp.start())
        for r in range(zpos_ref.shape[0]):
            zero_region(r, lambda cp: cp.wait())

    def wait_slot(sl):
        for _ in range(2):
            pltpu.make_async_copy(xbuf.at[sl], xbuf.at[sl], row_sem.at[sl]).wait()

    @pl.when(i >= 2)
    def _():
        wait_slot(slot)

    xbuf[slot] = x_ref[...].reshape(xbuf.shape[1:])
    _row_copies(pos_ref.at[0], idx_ref, idx_sem, 2, td // SUBLANES,
                lambda k, g, u, p: pltpu.make_async_copy(xbuf.at[slot, g, pl.ds(u, 1)],
                                                         xs_hbm.at[pl.ds(p, 1)], row_sem.at[slot]))

    @pl.when(i == n - 1)
    def _():
        wait_slot(slot)

        @pl.when(n >= 2)
        def _():
            wait_slot(1 - slot)


def _dispatch(zpos, zlen, pos, x, n_rows):
    t, d = x.shape
    td = COMBINE_TILE
    grid_spec = pltpu.PrefetchScalarGridSpec(
        num_scalar_prefetch=2,
        grid=(t // td,),
        in_specs=[pl.BlockSpec((1, 2, td), lambda i, zp, zl: (i, 0, 0)),
                  pl.BlockSpec((td, d), lambda i, zp, zl: (i, 0))],
        out_specs=pl.BlockSpec(memory_space=pl.ANY),
        scratch_shapes=[pltpu.VMEM((2, td // SUBLANES, SUBLANES, d), F32),
                        pltpu.VMEM((td, d), F32),
                        pltpu.SMEM((2, td), jnp.int32),
                        pltpu.SemaphoreType.DMA((2,)),
                        pltpu.SemaphoreType.DMA(()),
                        pltpu.SemaphoreType.DMA(())],
    )
    return pl.pallas_call(
        _dispatch_kernel,
        out_shape=jax.ShapeDtypeStruct((n_rows, d), F32),
        grid_spec=grid_spec,
        compiler_params=_cparams(("arbitrary",), 48),
        name="moe_dispatch",
    )(zpos, zlen, pos, x)


def _moe_kernel(te_ref, na_ref, x_ref, w1_ref, w3_ref, w2_ref, ys_ref, xb_ref, acc_ref):
    i = pl.program_id(0)
    c = pl.program_id(1)
    active = i < na_ref[0]

    @pl.when(jnp.logical_and(active, c == 0))
    def _():
        xb_ref[...] = x_ref[...].astype(BF16)
        acc_ref[...] = jnp.zeros(acc_ref.shape, F32)

    @pl.when(active)
    def _():
        acc_ref[...] += _swiglu_chunk(xb_ref[...], w1_ref[...], w3_ref[...], w2_ref[...])

    @pl.when(c == pl.num_programs(1) - 1)
    def _():
        ys_ref[...] = jnp.where(active, acc_ref[...], 0.0)


def _moe_experts(tile_expert, n_active, xs, n_tiles, tm, w1, w3, w2, n_chunks):
    d = xs.shape[1]
    f = w1.shape[2]
    fc = f // n_chunks

    def chunk(i, c, na):
        return jnp.where(i < na[0], c, n_chunks - 1)

    grid_spec = pltpu.PrefetchScalarGridSpec(
        num_scalar_prefetch=2,
        grid=(n_tiles, n_chunks),
        in_specs=[pl.BlockSpec((tm, d), lambda i, c, te, na: (jnp.minimum(i, na[0] - 1), 0)),
                  pl.BlockSpec((None, d, fc), lambda i, c, te, na: (te[i], 0, chunk(i, c, na))),
                  pl.BlockSpec((None, d, fc), lambda i, c, te, na: (te[i], 0, chunk(i, c, na))),
                  pl.BlockSpec((None, fc, d), lambda i, c, te, na: (te[i], chunk(i, c, na), 0))],
        out_specs=pl.BlockSpec((tm, d), lambda i, c, te, na: (i, 0)),
        scratch_shapes=[pltpu.VMEM((tm, d), BF16), pltpu.VMEM((tm, d), F32)],
    )
    return pl.pallas_call(
        _moe_kernel,
        out_shape=jax.ShapeDtypeStruct((n_tiles * tm, d), F32),
        grid_spec=grid_spec,
        compiler_params=_cparams(("arbitrary", "arbitrary"), 56),
        name="moe_experts",
    )(tile_expert, n_active, xs, w1, w3, w2)


def _combine_kernel(pos_ref, posn_ref, gate_ref, x_ref, ys_hbm, g_ref, b_ref, o_ref,
                    ybuf, idx_ref, row_sem, idx_sem, *, alpha):
    i = pl.program_id(0)
    slot = i % 2

    def gather(pos_vmem, sl):
        _row_copies(pos_vmem.at[0], idx_ref, idx_sem, 2, ybuf.shape[2],
                    lambda k, g, u, p: pltpu.make_async_copy(ys_hbm.at[pl.ds(p, 1)],
                                                             ybuf.at[sl, k, g, pl.ds(u, 1)], row_sem.at[sl]))

    @pl.when(i == 0)
    def _():
        gather(pos_ref, 0)

    pltpu.make_async_copy(ybuf.at[slot], ybuf.at[slot], row_sem.at[slot]).wait()

    @pl.when(i + 1 < pl.num_programs(0))
    def _():
        gather(posn_ref, 1 - slot)

    gates = gate_ref[...]
    f = (gates[:, 0:1] * ybuf[slot, 0].reshape(x_ref.shape)
         + gates[:, 1:2] * ybuf[slot, 1].reshape(x_ref.shape))
    o_ref[...] = _layer_norm(alpha * x_ref[...] + f, g_ref[...], b_ref[...])


def _combine(pos, gates, x, ys, g, b, alpha):
    t, d = x.shape
    tc = COMBINE_TILE
    n = t // tc
    return pl.pallas_call(
        functools.partial(_combine_kernel, alpha=alpha),
        out_shape=jax.ShapeDtypeStruct((t, d), F32),
        grid=(n,),
        in_specs=[pl.BlockSpec((1, 2, tc), lambda i: (i, 0, 0)),
                  pl.BlockSpec((1, 2, tc), lambda i: (jnp.minimum(i + 1, n - 1), 0, 0)),
                  pl.BlockSpec((tc, 2), lambda i: (i, 0)),
                  pl.BlockSpec((tc, d), lambda i: (i, 0)),
                  pl.BlockSpec(memory_space=pl.ANY),
                  pl.BlockSpec(g.shape, lambda i: (0, 0)),
                  pl.BlockSpec(b.shape, lambda i: (0, 0))],
        out_specs=pl.BlockSpec((tc, d), lambda i: (i, 0)),
        scratch_shapes=[pltpu.VMEM((2, 2, tc // SUBLANES, SUBLANES, d), F32),
                        pltpu.SMEM((2, tc), jnp.int32),
                        pltpu.SemaphoreType.DMA((2,)),
                        pltpu.SemaphoreType.DMA(())],
        compiler_params=_cparams(("arbitrary",), 48),
        name="moe_combine_ln",
    )(pos, pos, gates, x, ys, g, b)


def _moe_ffn(x, router_w, w1, w3, w2, g, b, alpha):
    t, d = x.shape
    tm = MOE_TILE
    routed, counts = _router(x, router_w)

    e1 = routed[0].astype(jnp.int32)
    e2 = routed[1].astype(jnp.int32)
    r1 = routed[4].astype(jnp.int32)
    r2 = routed[5].astype(jnp.int32)
    cnt = counts[:, 0].astype(jnp.int32)
    tiles_per_expert = (cnt + tm - 1) // tm
    tile_end = jnp.cumsum(tiles_per_expert)
    row_start = (tile_end - tiles_per_expert) * tm
    pos1 = row_start[e1] + r1
    pos2 = row_start[e2] + r2
    n_tiles = (2 * t) // tm + N_EXPERTS
    n_active = tile_end[-1:]
    tile_expert = jnp.minimum(
        jnp.searchsorted(tile_end, jnp.minimum(jnp.arange(n_tiles), n_active[0] - 1), side="right"),
        N_EXPERTS - 1).astype(jnp.int32)
    tc = COMBINE_TILE
    pos = jnp.stack([pos1.reshape(t // tc, tc), pos2.reshape(t // tc, tc)], axis=1)
    used = tile_end[-1:] * tm
    zpos = jnp.concatenate([(row_start + cnt) // SUBLANES * SUBLANES, used]).astype(jnp.int32)
    zend = jnp.concatenate([tile_end * tm, jnp.full((1,), n_tiles * tm, jnp.int32)]).astype(jnp.int32)
    xs = _dispatch(zpos, zend - zpos, pos, x, n_tiles * tm)
    ys = _moe_experts(tile_expert, n_active.astype(jnp.int32), xs, n_tiles, tm, w1, w3, w2, 4)

    gates = jnp.stack([routed[2], routed[3]], axis=1)
    return _combine(pos, gates, x, ys, g, b, alpha)


def kernel(x, w_in, w_out, gmlp_ws, gmlp_bs, lam_q1, lam_k1, lam_q2, lam_k2, diff_subln_g, rel_bias,
           pool_w, pool_scale, ln1_g, ln1_b, ln2_g, ln2_b, ffn_w1, ffn_w3, ffn_w2, router_w,
           moe_w1, moe_w3, moe_w2):
    batch, seq_len, d_model = x.shape
    depth = w_in.shape[0]
    d_a = w_out.shape[1] // 4
    d_b = H_B * 2 * DH_B
    d_c = pool_w.shape[1] * pool_w.shape[2]
    alpha = (2 * depth) ** 0.25
    t = batch * seq_len
    o_a = 2 * d_a
    o_q = o_a + d_b
    o_k = o_q + d_b
    o_v = o_k + d_b

    xt = x.reshape(t, d_model)
    bias = _attn_bias_tiles(rel_bias, ATTN_BLOCK)
    q_scale = DH_B ** -0.5 * math.log2(math.e)

    for l in range(depth):
        wl = w_in[l]
        w_cat = jnp.concatenate([wl[:, :o_a], wl[:, o_v:], wl[:, o_a:o_q] * q_scale, wl[:, o_q:o_v]],
                                axis=1).astype(BF16)
        wcat = jnp.transpose(gmlp_ws[l], (1, 0, 2)).reshape(CHUNK, H_A * CHUNK).astype(BF16)
        bmat = jnp.repeat(gmlp_bs[l].T, d_a // H_A, axis=1)
        pw = jax.scipy.linalg.block_diag(*[pool_w[l, gi] for gi in range(G_C)]).astype(BF16)
        ps = pool_scale[l].reshape(1, d_c)
        lam_init = 0.8 - 0.6 * math.exp(-0.3 * l)
        lam = (jnp.exp(jnp.sum(lam_q1[l] * lam_k1[l])) - jnp.exp(jnp.sum(lam_q2[l] * lam_k2[l]))
               + lam_init).reshape(1).astype(F32)
        gain = (diff_subln_g[l] * (1.0 - lam_init)).reshape(2 * DH_B, 1)

        qkv, ya, yc = _in_mix(xt, w_cat, wcat, bmat, pw, ps, seq_len, d_a, d_c)
        yb = _attention(qkv, bias, lam, gain, batch, seq_len)
        g1 = ln1_g[l].reshape(1, -1)
        b1 = ln1_b[l].reshape(1, -1)
        g2 = ln2_g[l].reshape(1, -1)
        b2 = ln2_b[l].reshape(1, -1)
        xt = _out_proj(xt, ya, yb, yc, w_out[l].astype(BF16), g1, b1, alpha)
        i = l // 2
        if l % 2 == 0:
            xt = _ffn(xt, ffn_w1[i].astype(BF16), ffn_w3[i].astype(BF16), ffn_w2[i].astype(BF16), g2, b2,
                      alpha, 2)
        else:
            xt = _moe_ffn(xt, router_w[i], moe_w1[i].astype(BF16), moe_w3[i].astype(BF16),
                          moe_w2[i].astype(BF16), g2, b2, alpha)
    return xt.reshape(batch, seq_len, d_model)
```

```python
import functools
import math

import jax
import jax.numpy as jnp
from jax import lax
from jax.experimental import pallas as pl
from jax.experimental.pallas import tpu as pltpu

F32 = jnp.float32
BF16 = jnp.bfloat16

H_A = 4
CHUNK = 128
H_B = 4
DH_B = 64
G_C = 4
POOL_WINDOWS = (2, 4, 8, 16)
POOL_HALO = 16
NUM_BUCKETS = 32
MAX_EXACT = NUM_BUCKETS // 2
MAX_DISTANCE = 128
N_EXPERTS = 8
EPS = 1e-5

LANES = 128
SUBLANES = 8
MIB = 1024 * 1024

ROW_TILE = 512
ATTN_BLOCK = 512
MOE_TILE = 1024
COMBINE_TILE = 256


def _cparams(semantics, vmem_mib):
    return pltpu.CompilerParams(dimension_semantics=semantics, vmem_limit_bytes=vmem_mib * MIB)


def _dot(a, b):
    return jnp.dot(a, b, preferred_element_type=F32)


def _layer_norm(r, g, b):
    mu = jnp.mean(r, axis=-1, keepdims=True)
    d = r - mu
    var = jnp.mean(d * d, axis=-1, keepdims=True)
    return d * lax.rsqrt(var + EPS) * g + b


def _split_bf16(a):
    hi = a.astype(BF16)
    lo = (a - hi.astype(F32)).astype(BF16)
    return hi, lo


def _in_mix_kernel(x_ref, w_ref, wcat_ref, bmat_ref, pw_ref, ps_ref,
                   qkv_ref, ya_ref, yc_ref, ext_ref, halo_ref, *, seq_len):
    i = pl.program_id(0)
    tm = x_ref.shape[0]
    d_a = ya_ref.shape[1]
    d_c = yc_ref.shape[1]
    hd = d_a // H_A
    gd = d_c // G_C
    n_mix = 2 * d_a + d_c

    xb = x_ref[...].astype(BF16)
    pf = _dot(xb, w_ref[:, :n_mix])
    qkv_ref[...] = _dot(xb, w_ref[:, n_mix:]).astype(BF16)

    z = pf[:, :2 * d_a]
    g = 0.5 * z * (1.0 + lax.erf(z * (1.0 / math.sqrt(2.0))))
    u = g[:, :d_a]
    v = g[:, d_a:]
    ri = lax.broadcasted_iota(jnp.int32, (d_a, d_a), 0) // hd
    ci = lax.broadcasted_iota(jnp.int32, (d_a, d_a), 1) // hd
    grp = jnp.where(ri == ci, 1.0, 0.0).astype(BF16)

    def group_mean(a):
        hi, lo = _split_bf16(a)
        return (_dot(hi, grp) + _dot(lo, grp)) * (1.0 / hd)

    mu = group_mean(v)
    dv = v - mu
    var = group_mean(dv * dv)
    vn = dv * lax.rsqrt(var + EPS)

    wr = lax.broadcasted_iota(jnp.int32, wcat_ref.shape, 0)
    wc = lax.broadcasted_iota(jnp.int32, wcat_ref.shape, 1) % CHUNK
    wcat = jnp.where(wc <= wr, wcat_ref[...], jnp.zeros_like(wcat_ref[...]))
    lane_head = lax.broadcasted_iota(jnp.int32, (CHUNK, d_a), 1) // hd
    bmat = bmat_ref[...]
    for ck in range(tm // CHUNK):
        rows = slice(ck * CHUNK, (ck + 1) * CHUNK)
        vn_c = vn[rows]
        stacked = jnp.concatenate(
            [jnp.where(lane_head == h, vn_c, 0.0).astype(BF16) for h in range(H_A)], axis=0)
        sg = _dot(wcat, stacked) + bmat
        ya_ref[rows, :] = (u[rows] * sg).astype(ya_ref.dtype)

    c = pf[:, 2 * d_a:]
    pos0 = (i * tm) % seq_len

    @pl.when(pos0 == 0)
    def _():
        halo_ref[...] = jnp.zeros(halo_ref.shape, F32)

    ext_ref[0:POOL_HALO, :] = halo_ref[...]
    ext_ref[POOL_HALO:, :] = c
    halo_ref[...] = c[tm - POOL_HALO:, :]
    lane_grp = lax.broadcasted_iota(jnp.int32, (tm, d_c), 1) // gd
    run = c
    wsum = jnp.zeros_like(c)
    win = jnp.zeros((tm, d_c), jnp.int32)
    k = 1
    for gi, w in enumerate(POOL_WINDOWS):
        while k < w:
            run = run + ext_ref[pl.ds(POOL_HALO - k, tm), :]
            k += 1
        wsum = jnp.where(lane_grp == gi, run, wsum)
        win = jnp.where(lane_grp == gi, w, win)
    pos = pos0 + lax.broadcasted_iota(jnp.int32, (tm, d_c), 0)
    cnt = jnp.minimum(pos + 1, win).astype(F32)
    y = (wsum / cnt - c).astype(BF16)
    yc_ref[...] = (_dot(y, pw_ref[...]) * ps_ref[...]).astype(yc_ref.dtype)


def _in_mix(x, w, wcat, bmat, pw, ps, seq_len, d_a, d_c):
    t, d = x.shape
    n = w.shape[1]
    n_qkv = n - 2 * d_a - d_c
    tm = ROW_TILE
    full = lambda a: pl.BlockSpec(a.shape, lambda i: (0, 0))
    return pl.pallas_call(
        functools.partial(_in_mix_kernel, seq_len=seq_len),
        out_shape=(jax.ShapeDtypeStruct((t, n_qkv), BF16), jax.ShapeDtypeStruct((t, d_a), BF16),
                   jax.ShapeDtypeStruct((t, d_c), BF16)),
        grid=(t // tm,),
        in_specs=[pl.BlockSpec((tm, d), lambda i: (i, 0)), full(w), full(wcat), full(bmat), full(pw), full(ps)],
        out_specs=(pl.BlockSpec((tm, n_qkv), lambda i: (i, 0)),
                   pl.BlockSpec((tm, d_a), lambda i: (i, 0)),
                   pl.BlockSpec((tm, d_c), lambda i: (i, 0))),
        scratch_shapes=[pltpu.VMEM((POOL_HALO + tm, d_c), F32), pltpu.VMEM((POOL_HALO, d_c), F32)],
        compiler_params=_cparams(("arbitrary",), 48),
        name="in_proj_mix_ac",
    )(x, w, wcat, bmat, pw, ps)


def _attn_kernel(lam_ref, q_ref, k_ref, v_ref, bias_ref, g_ref, o_ref,
                 qs_ref, m_ref, l_ref, acc_ref, s0_ref, s1_ref, c0_ref, c1_ref):
    i = pl.program_id(2)
    tb, dv = q_ref.shape

    q = q_ref[...]
    lane = lax.broadcasted_iota(jnp.int32, (tb, dv), 1)
    qs_ref[0:tb, :] = jnp.where(lane < DH_B, q, jnp.zeros_like(q))
    qs_ref[tb:, :] = jnp.where(lane >= DH_B, q, jnp.zeros_like(q))
    m_ref[...] = jnp.full(m_ref.shape, -jnp.inf, F32)
    l_ref[...] = jnp.zeros(l_ref.shape, F32)
    acc_ref[...] = jnp.zeros(acc_ref.shape, F32)

    bufs = ((s0_ref, c0_ref), (s1_ref, c1_ref))

    def scores(buf, j, bias=None):
        s_ref, cm_ref = buf
        kj = k_ref[pl.ds(pl.multiple_of(j * tb, tb), tb), :]
        s = lax.dot_general(kj, qs_ref[...], (((1,), (1,)), ((), ())), preferred_element_type=F32)
        if bias is not None:
            s = s + jnp.concatenate([bias, bias], axis=1)
        s_ref[...] = s
        cm_ref[...] = jnp.max(s, axis=0, keepdims=True)

    def consume(buf, j, bias=None):
        s_ref, cm_ref = buf
        vjt = jnp.transpose(v_ref[pl.ds(pl.multiple_of(j * tb, tb), tb), :])
        s = s_ref[...]
        cm = cm_ref[...]
        if bias is not None:
            s = s + jnp.concatenate([bias, bias], axis=1)
            cm = jnp.max(s, axis=0, keepdims=True)
        m_prev = m_ref[...]
        m_next = jnp.maximum(m_prev, cm)
        alpha = jnp.exp2(m_prev - m_next)
        p = jnp.exp2(s - m_next)
        l_ref[...] = alpha * l_ref[...] + jnp.sum(p, axis=0, keepdims=True)
        m_ref[...] = m_next
        acc_ref[...] = alpha * acc_ref[...] + _dot(vjt, p.astype(BF16))

    def pipelined(base, n):
        for u in range(n):
            scores(bufs[(u + 1) % 2], base + u + 1)
            consume(bufs[u % 2], base + u)

    n_far = jnp.maximum(i - 1, 0)

    @pl.when(i == 0)
    def _():
        scores(bufs[0], 0, bias_ref[1])
        consume(bufs[0], 0)

    @pl.when(i == 1)
    def _():
        scores(bufs[0], 0, bias_ref[0])
        scores(bufs[1], 1, bias_ref[1])
        consume(bufs[0], 0)
        consume(bufs[1], 1)

    @pl.when(i >= 2)
    def _():
        scores(bufs[0], 0)

        def oct_body(t, carry):
            pipelined(8 * t, 8)
            return carry

        lax.fori_loop(0, n_far // 8, oct_body, 0)

        def quad_body(t, carry):
            pipelined((n_far // 8) * 8 + 4 * t, 4)
            return carry

        lax.fori_loop(0, (n_far % 8) // 4, quad_body, 0)

        def pair_body(t, carry):
            pipelined((n_far // 4) * 4 + 2 * t, 2)
            return carry

        lax.fori_loop(0, (n_far % 4) // 2, pair_body, 0)

        @pl.when(n_far % 2 == 0)
        def _():
            scores(bufs[1], i, bias_ref[1])
            consume(bufs[0], i - 1, bias_ref[0])
            consume(bufs[1], i)

        @pl.when(n_far % 2 == 1)
        def _():
            scores(bufs[1], i - 1, bias_ref[0])
            consume(bufs[0], i - 2)
            scores(bufs[0], i, bias_ref[1])
            consume(bufs[1], i - 1)
            consume(bufs[0], i)

    o = acc_ref[...] / l_ref[...]
    d = o[:, :tb] - lam_ref[0] * o[:, tb:]
    ms = jnp.mean(d * d, axis=0, keepdims=True)
    y = d * lax.rsqrt(ms + EPS) * g_ref[...]
    o_ref[...] = jnp.transpose(y).astype(o_ref.dtype)


def _attention(qkv, bias, lam, gain, batch, seq_len):
    t = qkv.shape[0]
    tb = ATTN_BLOCK
    nq = seq_len // tb
    dv = 2 * DH_B
    return pl.pallas_call(
        _attn_kernel,
        out_shape=jax.ShapeDtypeStruct((t, H_B * dv), BF16),
        grid=(batch, H_B, nq),
        in_specs=[pl.BlockSpec(memory_space=pltpu.SMEM),
                  pl.BlockSpec((tb, dv), lambda b, h, i: (b * nq + i, h)),
                  pl.BlockSpec((seq_len, dv), lambda b, h, i: (b, H_B + h)),
                  pl.BlockSpec((seq_len, dv), lambda b, h, i: (b, 2 * H_B + h)),
                  pl.BlockSpec((None, 2, tb, tb), lambda b, h, i: (h, 0, 0, 0)),
                  pl.BlockSpec((dv, 1), lambda b, h, i: (0, 0))],
        out_specs=pl.BlockSpec((tb, dv), lambda b, h, i: (b * nq + i, h)),
        scratch_shapes=[pltpu.VMEM((2 * tb, dv), BF16),
                        pltpu.VMEM((1, 2 * tb), F32),
                        pltpu.VMEM((1, 2 * tb), F32),
                        pltpu.VMEM((dv, 2 * tb), F32),
                        pltpu.VMEM((tb, 2 * tb), F32),
                        pltpu.VMEM((tb, 2 * tb), F32),
                        pltpu.VMEM((1, 2 * tb), F32),
                        pltpu.VMEM((1, 2 * tb), F32)],
        compiler_params=_cparams(("arbitrary", "arbitrary", "arbitrary"), 56),
        name="diff_attn",
    )(lam, qkv, qkv, qkv, bias, gain)


def _bias_kernel(rel_ref, o_ref):
    h = pl.program_id(0)
    tb = o_ref.shape[-1]
    kk = lax.broadcasted_iota(jnp.int32, (tb, tb), 0)
    qq = lax.broadcasted_iota(jnp.int32, (tb, tb), 1)
    far = rel_ref[(NUM_BUCKETS - 1) * H_B + h]
    for w, off in enumerate((tb, 0)):
        dist = off + qq - kk
        n = jnp.maximum(dist, 0)
        nf = jnp.maximum(n, 1).astype(F32)
        large = MAX_EXACT + (jnp.log(nf / MAX_EXACT) / math.log(MAX_DISTANCE / MAX_EXACT)
                             * (NUM_BUCKETS - MAX_EXACT)).astype(jnp.int32)
        bucket = jnp.where(n < MAX_EXACT, n, jnp.minimum(large, NUM_BUCKETS - 1))
        b = jnp.zeros((tb, tb), F32)
        for k in range(NUM_BUCKETS):
            b = jnp.where(bucket == k, rel_ref[k * H_B + h], b)
        o_ref[w] = jnp.where(dist >= 0, (b - far) * math.log2(math.e), -jnp.inf)


def _attn_bias_tiles(rel_bias, tb):
    return pl.pallas_call(
        _bias_kernel,
        out_shape=jax.ShapeDtypeStruct((H_B, 2, tb, tb), F32),
        grid=(H_B,),
        in_specs=[pl.BlockSpec(memory_space=pltpu.SMEM)],
        out_specs=pl.BlockSpec((None, 2, tb, tb), lambda h: (h, 0, 0, 0)),
        compiler_params=_cparams(("arbitrary",), 48),
        name="attn_bias",
    )(rel_bias.reshape(NUM_BUCKETS * H_B))


def _out_proj_kernel(x_ref, ya_ref, yb_ref, yc_ref, w_ref, g_ref, b_ref, o_ref, *, alpha):
    d_a = ya_ref.shape[1]
    d_b = yb_ref.shape[1]
    h = (_dot(ya_ref[...], w_ref[0:d_a, :]) + _dot(yb_ref[...], w_ref[d_a:d_a + d_b, :])
         + _dot(yc_ref[...], w_ref[d_a + d_b:, :]))
    o_ref[...] = _layer_norm(alpha * x_ref[...] + h, g_ref[...], b_ref[...])


def _out_proj(x, ya, yb, yc, w, g, b, alpha):
    t, d = x.shape
    tm = ROW_TILE
    row = lambda n: pl.BlockSpec((tm, n), lambda i: (i, 0))
    full = lambda a: pl.BlockSpec(a.shape, lambda i: (0, 0))
    return pl.pallas_call(
        functools.partial(_out_proj_kernel, alpha=alpha),
        out_shape=jax.ShapeDtypeStruct((t, d), F32),
        grid=(t // tm,),
        in_specs=[row(d), row(ya.shape[1]), row(yb.shape[1]), row(yc.shape[1]), full(w), full(g), full(b)],
        out_specs=row(d),
        compiler_params=_cparams(("arbitrary",), 48),
        name="out_proj_ln",
    )(x, ya, yb, yc, w, g, b)


def _swiglu_chunk(xb, w1, w3, w2):
    h1 = _dot(xb, w1)
    h3 = _dot(xb, w3)
    gate = h1 * (1.0 / (1.0 + jnp.exp(-h1)))
    return _dot((gate * h3).astype(BF16), w2)


def _ffn_kernel(x_ref, w1_ref, w3_ref, w2_ref, g_ref, b_ref, o_ref, xb_ref, acc_ref, *, alpha):
    c = pl.program_id(1)

    @pl.when(c == 0)
    def _():
        xb_ref[...] = x_ref[...].astype(BF16)
        acc_ref[...] = jnp.zeros(acc_ref.shape, F32)

    acc_ref[...] += _swiglu_chunk(xb_ref[...], w1_ref[...], w3_ref[...], w2_ref[...])

    @pl.when(c == pl.num_programs(1) - 1)
    def _():
        o_ref[...] = _layer_norm(alpha * x_ref[...] + acc_ref[...], g_ref[...], b_ref[...])


def _ffn(x, w1, w3, w2, g, b, alpha, n_chunks):
    t, d = x.shape
    f = w1.shape[1]
    fc = f // n_chunks
    tm = ROW_TILE
    return pl.pallas_call(
        functools.partial(_ffn_kernel, alpha=alpha),
        out_shape=jax.ShapeDtypeStruct((t, d), F32),
        grid=(t // tm, n_chunks),
        in_specs=[pl.BlockSpec((tm, d), lambda i, c: (i, 0)),
                  pl.BlockSpec((d, fc), lambda i, c: (0, c)),
                  pl.BlockSpec((d, fc), lambda i, c: (0, c)),
                  pl.BlockSpec((fc, d), lambda i, c: (c, 0)),
                  pl.BlockSpec(g.shape, lambda i, c: (0, 0)),
                  pl.BlockSpec(b.shape, lambda i, c: (0, 0))],
        out_specs=pl.BlockSpec((tm, d), lambda i, c: (i, 0)),
        scratch_shapes=[pltpu.VMEM((tm, d), BF16), pltpu.VMEM((tm, d), F32)],
        compiler_params=_cparams(("arbitrary", "arbitrary"), 56),
        name="ffn_ln",
    )(x, w1, w3, w2, g, b)


def _route_tile(x, wh_ref, wl_ref, r_ref, cnt_ref, carry_ref):
    i = pl.program_id(0)
    tr = x.shape[0]
    ne = N_EXPERTS

    @pl.when(i == 0)
    def _():
        carry_ref[...] = jnp.zeros(carry_ref.shape, F32)

    xh, xl = _split_bf16(x)
    logits = _dot(xh, wh_ref[...]) + _dot(xl, wh_ref[...]) + _dot(xh, wl_ref[...])
    lt = jnp.transpose(logits)[0:ne, :]

    eio = lax.broadcasted_iota(jnp.int32, (ne, tr), 0)
    v1 = jnp.max(lt, axis=0, keepdims=True)
    i1 = jnp.min(jnp.where(lt == v1, eio, ne), axis=0, keepdims=True)
    oh1 = eio == i1
    lt2 = jnp.where(oh1, -jnp.inf, lt)
    v2 = jnp.max(lt2, axis=0, keepdims=True)
    i2 = jnp.min(jnp.where(lt2 == v2, eio, ne), axis=0, keepdims=True)
    oh2 = eio == i2
    e = jnp.exp(v2 - v1)
    g1 = 1.0 / (1.0 + e)
    g2 = e / (1.0 + e)

    member = jnp.where(oh1, 1.0, jnp.where(oh2, 1.0, 0.0))
    tri_r = lax.broadcasted_iota(jnp.int32, (tr, tr), 0)
    tri_c = lax.broadcasted_iota(jnp.int32, (tr, tr), 1)
    upper = jnp.where(tri_r < tri_c, 1.0, 0.0).astype(BF16)
    ranks = _dot(member.astype(BF16), upper) + carry_ref[:, 0:1]
    r1 = jnp.sum(jnp.where(oh1, ranks, 0.0), axis=0, keepdims=True)
    r2 = jnp.sum(jnp.where(oh2, ranks, 0.0), axis=0, keepdims=True)
    carry_ref[...] = carry_ref[...] + jnp.sum(member, axis=1, keepdims=True)
    cnt_ref[...] = carry_ref[...]

    rows = [i1.astype(F32), i2.astype(F32), g1, g2, r1, r2]
    out = jnp.zeros((ne, tr), F32)
    for k, row in enumerate(rows):
        out = jnp.where(eio == k, row, out)
    r_ref[...] = out


def _router_kernel(x_ref, wh_ref, wl_ref, r_ref, cnt_ref, carry_ref):
    _route_tile(x_ref[...], wh_ref, wl_ref, r_ref, cnt_ref, carry_ref)


def _router(x, router_w):
    t, d = x.shape
    tr = ROW_TILE
    rw = jnp.zeros((d, LANES), F32).at[:, :N_EXPERTS].set(router_w)
    wh = rw.astype(BF16)
    wl = (rw - wh.astype(F32)).astype(BF16)
    return pl.pallas_call(
        _router_kernel,
        out_shape=(jax.ShapeDtypeStruct((N_EXPERTS, t), F32), jax.ShapeDtypeStruct((N_EXPERTS, LANES), F32)),
        grid=(t // tr,),
        in_specs=[pl.BlockSpec((tr, d), lambda i: (i, 0)),
                  pl.BlockSpec(wh.shape, lambda i: (0, 0)),
                  pl.BlockSpec(wl.shape, lambda i: (0, 0))],
        out_specs=(pl.BlockSpec((N_EXPERTS, tr), lambda i: (0, i)),
                   pl.BlockSpec((N_EXPERTS, LANES), lambda i: (0, 0))),
        scratch_shapes=[pltpu.VMEM((N_EXPERTS, LANES), F32)],
        compiler_params=_cparams(("arbitrary",), 48),
        name="router",
    )(x, wh, wl)


def _row_copies(idx_vmem, idx_smem, idx_sem, n_lists, make_copy):
    cp = pltpu.make_async_copy(idx_vmem, idx_smem, idx_sem)
    cp.start()
    cp.wait()
    rows = idx_smem.shape[0]
    per_list = rows * LANES // n_lists
    for r in range(rows):
        for c in range(LANES):
            k, j = divmod(r * LANES + c, per_list)
            make_copy(k, j // SUBLANES, j % SUBLANES, idx_smem[r, c]).start(priority=c % 2)


def _dispatch_kernel(zpos_ref, zlen_ref, pos_ref, x_ref, xs_hbm, xbuf, zbuf, idx_ref, row_sem, idx_sem, zero_sem):
    i = pl.program_id(0)
    n = pl.num_programs(0)
    td = x_ref.shape[0]
    slot = i % 2
    zrows = zbuf.shape[0]

    def zero_region(r, act):
        start, length = zpos_ref[r], zlen_ref[r]
        n_big = length // zrows
        rest = start + n_big * zrows

        def big(q, carry):
            act(pltpu.make_async_copy(
                zbuf, xs_hbm.at[pl.ds(pl.multiple_of(start + q * zrows, SUBLANES), zrows)], zero_sem))
            return carry

        def small(q, carry):
            act(pltpu.make_async_copy(
                zbuf.at[0:SUBLANES], xs_hbm.at[pl.ds(pl.multiple_of(rest + q * SUBLANES, SUBLANES), SUBLANES)],
                zero_sem))
            return carry

        lax.fori_loop(0, n_big, big, 0)
        lax.fori_loop(0, (length - n_big * zrows) // SUBLANES, small, 0)

    @pl.when(i == 0)
    def _():
        zbuf[...] = jnp.zeros(zbuf.shape, F32)
        for r in range(zpos_ref.shape[0]):
            zero_region(r, lambda cp: cp.start())
        for r in range(zpos_ref.shape[0]):
            zero_region(r, lambda cp: cp.wait())

    def wait_slot(sl):
        for _ in range(2):
            pltpu.make_async_copy(xbuf.at[sl], xbuf.at[sl], row_sem.at[sl]).wait()

    @pl.when(i >= 2)
    def _():
        wait_slot(slot)

    for sl in range(2):
        @pl.when(slot == sl)
        def _(sl=sl):
            xbuf[sl] = x_ref[...].reshape(xbuf.shape[1:])
            _row_copies(pos_ref.at[0], idx_ref, idx_sem, 2,
                        lambda k, g, u, p: pltpu.make_async_copy(xbuf.at[sl, g, pl.ds(u, 1)],
                                                                 xs_hbm.at[pl.ds(p, 1)], row_sem.at[sl]))

    @pl.when(i == n - 1)
    def _():
        wait_slot(slot)

        @pl.when(n >= 2)
        def _():
            wait_slot(1 - slot)


def _dispatch(zpos, zlen, pos, x, n_rows):
    t, d = x.shape
    td = COMBINE_TILE
    grid_spec = pltpu.PrefetchScalarGridSpec(
        num_scalar_prefetch=2,
        grid=(t // td,),
        in_specs=[pl.BlockSpec((1, 2 * td // LANES, LANES), lambda i, zp, zl: (i, 0, 0)),
                  pl.BlockSpec((td, d), lambda i, zp, zl: (i, 0))],
        out_specs=pl.BlockSpec(memory_space=pl.ANY),
        scratch_shapes=[pltpu.VMEM((2, td // SUBLANES, SUBLANES, d), F32),
                        pltpu.VMEM((td, d), F32),
                        pltpu.SMEM((2 * td // LANES, LANES), jnp.int32),
                        pltpu.SemaphoreType.DMA((2,)),
                        pltpu.SemaphoreType.DMA(()),
                        pltpu.SemaphoreType.DMA(())],
    )
    return pl.pallas_call(
        _dispatch_kernel,
        out_shape=jax.ShapeDtypeStruct((n_rows, d), F32),
        grid_spec=grid_spec,
        compiler_params=_cparams(("arbitrary",), 48),
        name="moe_dispatch",
    )(zpos, zlen, pos, x)


def _moe_kernel(te_ref, na_ref, x_ref, w1_ref, w3_ref, w2_ref, ys_ref, xb_ref, acc_ref):
    i = pl.program_id(0)
    c = pl.program_id(1)
    active = i < na_ref[0]

    @pl.when(jnp.logical_and(active, c == 0))
    def _():
        xb_ref[...] = x_ref[...].astype(BF16)
        acc_ref[...] = jnp.zeros(acc_ref.shape, F32)

    @pl.when(active)
    def _():
        acc_ref[...] += _swiglu_chunk(xb_ref[...], w1_ref[...], w3_ref[...], w2_ref[...])

    @pl.when(c == pl.num_programs(1) - 1)
    def _():
        ys_ref[...] = jnp.where(active, acc_ref[...], 0.0)


def _moe_experts(tile_expert, n_active, xs, n_tiles, tm, w1, w3, w2, n_chunks):
    d = xs.shape[1]
    f = w1.shape[2]
    fc = f // n_chunks

    def chunk(i, c, na):
        return jnp.where(i < na[0], c, n_chunks - 1)

    grid_spec = pltpu.PrefetchScalarGridSpec(
        num_scalar_prefetch=2,
        grid=(n_tiles, n_chunks),
        in_specs=[pl.BlockSpec((tm, d), lambda i, c, te, na: (jnp.minimum(i, na[0] - 1), 0)),
                  pl.BlockSpec((None, d, fc), lambda i, c, te, na: (te[i], 0, chunk(i, c, na))),
                  pl.BlockSpec((None, d, fc), lambda i, c, te, na: (te[i], 0, chunk(i, c, na))),
                  pl.BlockSpec((None, fc, d), lambda i, c, te, na: (te[i], chunk(i, c, na), 0))],
        out_specs=pl.BlockSpec((tm, d), lambda i, c, te, na: (i, 0)),
        scratch_shapes=[pltpu.VMEM((tm, d), BF16), pltpu.VMEM((tm, d), F32)],
    )
    return pl.pallas_call(
        _moe_kernel,
        out_shape=jax.ShapeDtypeStruct((n_tiles * tm, d), F32),
        grid_spec=grid_spec,
        compiler_params=_cparams(("arbitrary", "arbitrary"), 56),
        name="moe_experts",
    )(tile_expert, n_active, xs, w1, w3, w2)


def _combine_kernel(pos_ref, gate_ref, x_ref, ys_hbm, g_ref, b_ref, o_ref,
                    ybuf, idx_ref, row_sem, idx_sem, *, alpha):
    s = pl.program_id(0)
    n = pl.num_programs(0) - 1

    for sl in range(2):
        @pl.when(jnp.logical_and(s < n, s % 2 == sl))
        def _(sl=sl):
            _row_copies(pos_ref.at[0], idx_ref, idx_sem, 2,
                        lambda k, g, u, p: pltpu.make_async_copy(ys_hbm.at[pl.ds(p, 1)],
                                                                 ybuf.at[sl, k, g, pl.ds(u, 1)], row_sem.at[sl]))

    @pl.when(s > 0)
    def _():
        slot = (s - 1) % 2
        pltpu.make_async_copy(ybuf.at[slot], ybuf.at[slot], row_sem.at[slot]).wait()
        gates = gate_ref[...]
        f = (gates[:, 0:1] * ybuf[slot, 0].reshape(x_ref.shape)
             + gates[:, 1:2] * ybuf[slot, 1].reshape(x_ref.shape))
        o_ref[...] = _layer_norm(alpha * x_ref[...] + f, g_ref[...], b_ref[...])


def _combine(pos, gates, x, ys, g, b, alpha):
    t, d = x.shape
    tc = COMBINE_TILE
    n = t // tc
    cur = lambda s: (jnp.minimum(s, n - 1), 0, 0)
    done = lambda s: (jnp.maximum(s - 1, 0), 0)
    return pl.pallas_call(
        functools.partial(_combine_kernel, alpha=alpha),
        out_shape=jax.ShapeDtypeStruct((t, d), F32),
        grid=(n + 1,),
        in_specs=[pl.BlockSpec((1, 2 * tc // LANES, LANES), cur),
                  pl.BlockSpec((tc, 2), done),
                  pl.BlockSpec((tc, d), done),
                  pl.BlockSpec(memory_space=pl.ANY),
                  pl.BlockSpec(g.shape, lambda s: (0, 0)),
                  pl.BlockSpec(b.shape, lambda s: (0, 0))],
        out_specs=pl.BlockSpec((tc, d), done),
        scratch_shapes=[pltpu.VMEM((2, 2, tc // SUBLANES, SUBLANES, d), F32),
                        pltpu.SMEM((2 * tc // LANES, LANES), jnp.int32),
                        pltpu.SemaphoreType.DMA((2,)),
                        pltpu.SemaphoreType.DMA(())],
        compiler_params=_cparams(("arbitrary",), 48),
        name="moe_combine_ln",
    )(pos, gates, x, ys, g, b)


def _moe_ffn(x, router_w, w1, w3, w2, g, b, alpha):
    t, d = x.shape
    tm = MOE_TILE
    routed, counts = _router(x, router_w)

    e1 = routed[0].astype(jnp.int32)
    e2 = routed[1].astype(jnp.int32)
    r1 = routed[4].astype(jnp.int32)
    r2 = routed[5].astype(jnp.int32)
    cnt = counts[:, 0].astype(jnp.int32)
    tiles_per_expert = (cnt + tm - 1) // tm
    tile_end = jnp.cumsum(tiles_per_expert)
    row_start = (tile_end - tiles_per_expert) * tm
    pos1 = row_start[e1] + r1
    pos2 = row_start[e2] + r2
    n_tiles = (2 * t) // tm + N_EXPERTS
    n_active = tile_end[-1:]
    tile_ids = jnp.minimum(jnp.arange(n_tiles), n_active[0] - 1)
    tile_expert = jnp.minimum(jnp.sum(tile_ids[:, None] >= tile_end[None, :], axis=1),
                              N_EXPERTS - 1).astype(jnp.int32)
    tc = COMBINE_TILE
    pos = jnp.stack([pos1.reshape(t // tc, tc), pos2.reshape(t // tc, tc)], axis=1).reshape(
        t // tc, 2 * tc // LANES, LANES)
    used = tile_end[-1:] * tm
    zpos = jnp.concatenate([(row_start + cnt) // SUBLANES * SUBLANES, used]).astype(jnp.int32)
    zend = jnp.concatenate([tile_end * tm, jnp.full((1,), n_tiles * tm, jnp.int32)]).astype(jnp.int32)
    xs = _dispatch(zpos, zend - zpos, pos, x, n_tiles * tm)
    ys = _moe_experts(tile_expert, n_active.astype(jnp.int32), xs, n_tiles, tm, w1, w3, w2, 4)

    gates = jnp.stack([routed[2], routed[3]], axis=1)
    return _combine(pos, gates, x, ys, g, b, alpha)


def kernel(x, w_in, w_out, gmlp_ws, gmlp_bs, lam_q1, lam_k1, lam_q2, lam_k2, diff_subln_g, rel_bias,
           pool_w, pool_scale, ln1_g, ln1_b, ln2_g, ln2_b, ffn_w1, ffn_w3, ffn_w2, router_w,
           moe_w1, moe_w3, moe_w2):
    batch, seq_len, d_model = x.shape
    depth = w_in.shape[0]
    d_a = w_out.shape[1] // 4
    d_b = H_B * 2 * DH_B
    d_c = pool_w.shape[1] * pool_w.shape[2]
    alpha = (2 * depth) ** 0.25
    t = batch * seq_len
    o_a = 2 * d_a
    o_q = o_a + d_b
    o_k = o_q + d_b
    o_v = o_k + d_b

    xt = x.reshape(t, d_model)
    bias = _attn_bias_tiles(rel_bias, ATTN_BLOCK)
    q_scale = DH_B ** -0.5 * math.log2(math.e)

    for l in range(depth):
        wl = w_in[l]
        w_cat = jnp.concatenate([wl[:, :o_a], wl[:, o_v:], wl[:, o_a:o_q] * q_scale, wl[:, o_q:o_v]],
                                axis=1).astype(BF16)
        wcat = jnp.transpose(gmlp_ws[l], (1, 0, 2)).reshape(CHUNK, H_A * CHUNK).astype(BF16)
        bmat = jnp.repeat(gmlp_bs[l].T, d_a // H_A, axis=1)
        pw = jax.scipy.linalg.block_diag(*[pool_w[l, gi] for gi in range(G_C)]).astype(BF16)
        ps = pool_scale[l].reshape(1, d_c)
        lam_init = 0.8 - 0.6 * math.exp(-0.3 * l)
        lam = (jnp.exp(jnp.sum(lam_q1[l] * lam_k1[l])) - jnp.exp(jnp.sum(lam_q2[l] * lam_k2[l]))
               + lam_init).reshape(1).astype(F32)
        gain = (diff_subln_g[l] * (1.0 - lam_init)).reshape(2 * DH_B, 1)

        qkv, ya, yc = _in_mix(xt, w_cat, wcat, bmat, pw, ps, seq_len, d_a, d_c)
        yb = _attention(qkv, bias, lam, gain, batch, seq_len)
        g1 = ln1_g[l].reshape(1, -1)
        b1 = ln1_b[l].reshape(1, -1)
        g2 = ln2_g[l].reshape(1, -1)
        b2 = ln2_b[l].reshape(1, -1)
        xt = _out_proj(xt, ya, yb, yc, w_out[l].astype(BF16), g1, b1, alpha)
        i = l // 2
        if l % 2 == 0:
            xt = _ffn(xt, ffn_w1[i].astype(BF16), ffn_w3[i].astype(BF16), ffn_w2[i].astype(BF16), g2, b2,
                      alpha, 2)
        else:
            xt = _moe_ffn(xt, router_w[i], moe_w1[i].astype(BF16), moe_w3[i].astype(BF16),
                          moe_w2[i].astype(BF16), g2, b2, alpha)
    return xt.reshape(batch, seq_len, d_model)
```

```python
import functools
import math

import jax
import jax.numpy as jnp
from jax import lax
from jax.experimental import pallas as pl
from jax.experimental.pallas import tpu as pltpu

F32 = jnp.float32
BF16 = jnp.bfloat16

H_A = 4
CHUNK = 128
H_B = 4
DH_B = 64
G_C = 4
POOL_WINDOWS = (2, 4, 8, 16)
POOL_HALO = 16
NUM_BUCKETS = 32
MAX_EXACT = NUM_BUCKETS // 2
MAX_DISTANCE = 128
N_EXPERTS = 8
EPS = 1e-5

LANES = 128
SUBLANES = 8
MIB = 1024 * 1024

ROW_TILE = 512
ATTN_BLOCK = 512
MOE_TILE = 512
COMBINE_TILE = 256


def _cparams(semantics, vmem_mib):
    return pltpu.CompilerParams(dimension_semantics=semantics, vmem_limit_bytes=vmem_mib * MIB)


def _dot(a, b):
    return jnp.dot(a, b, preferred_element_type=F32)


def _layer_norm(r, g, b):
    mu = jnp.mean(r, axis=-1, keepdims=True)
    d = r - mu
    var = jnp.mean(d * d, axis=-1, keepdims=True)
    return d * lax.rsqrt(var + EPS) * g + b


def _split_bf16(a):
    hi = a.astype(BF16)
    lo = (a - hi.astype(F32)).astype(BF16)
    return hi, lo


def _in_mix_kernel(x_ref, w_ref, wcat_ref, bmat_ref, pw_ref, ps_ref,
                   qkv_ref, ya_ref, yc_ref, ext_ref, halo_ref, *, seq_len):
    i = pl.program_id(0)
    tm = x_ref.shape[0]
    d_a = ya_ref.shape[1]
    d_c = yc_ref.shape[1]
    hd = d_a // H_A
    gd = d_c // G_C
    n_mix = 2 * d_a + d_c

    xb = x_ref[...].astype(BF16)
    pf = _dot(xb, w_ref[:, :n_mix])
    qkv_ref[...] = _dot(xb, w_ref[:, n_mix:]).astype(BF16)

    z = pf[:, :2 * d_a]
    g = 0.5 * z * (1.0 + lax.erf(z * (1.0 / math.sqrt(2.0))))
    u = g[:, :d_a]
    v = g[:, d_a:]
    ri = lax.broadcasted_iota(jnp.int32, (d_a, d_a), 0) // hd
    ci = lax.broadcasted_iota(jnp.int32, (d_a, d_a), 1) // hd
    grp = jnp.where(ri == ci, 1.0, 0.0).astype(BF16)

    def group_mean(a):
        hi, lo = _split_bf16(a)
        return (_dot(hi, grp) + _dot(lo, grp)) * (1.0 / hd)

    mu = group_mean(v)
    dv = v - mu
    var = group_mean(dv * dv)
    vn = dv * lax.rsqrt(var + EPS)

    wr = lax.broadcasted_iota(jnp.int32, wcat_ref.shape, 0)
    wc = lax.broadcasted_iota(jnp.int32, wcat_ref.shape, 1) % CHUNK
    wcat = jnp.where(wc <= wr, wcat_ref[...], jnp.zeros_like(wcat_ref[...]))
    lane_head = lax.broadcasted_iota(jnp.int32, (CHUNK, d_a), 1) // hd
    bmat = bmat_ref[...]
    for ck in range(tm // CHUNK):
        rows = slice(ck * CHUNK, (ck + 1) * CHUNK)
        vn_c = vn[rows]
        stacked = jnp.concatenate(
            [jnp.where(lane_head == h, vn_c, 0.0).astype(BF16) for h in range(H_A)], axis=0)
        sg = _dot(wcat, stacked) + bmat
        ya_ref[rows, :] = (u[rows] * sg).astype(ya_ref.dtype)

    c = pf[:, 2 * d_a:]
    pos0 = (i * tm) % seq_len

    @pl.when(pos0 == 0)
    def _():
        halo_ref[...] = jnp.zeros(halo_ref.shape, F32)

    ext_ref[0:POOL_HALO, :] = halo_ref[...]
    ext_ref[POOL_HALO:, :] = c
    halo_ref[...] = c[tm - POOL_HALO:, :]
    lane_grp = lax.broadcasted_iota(jnp.int32, (tm, d_c), 1) // gd
    run = c
    wsum = jnp.zeros_like(c)
    win = jnp.zeros((tm, d_c), jnp.int32)
    k = 1
    for gi, w in enumerate(POOL_WINDOWS):
        while k < w:
            run = run + ext_ref[pl.ds(POOL_HALO - k, tm), :]
            k += 1
        wsum = jnp.where(lane_grp == gi, run, wsum)
        win = jnp.where(lane_grp == gi, w, win)
    pos = pos0 + lax.broadcasted_iota(jnp.int32, (tm, d_c), 0)
    cnt = jnp.minimum(pos + 1, win).astype(F32)
    y = (wsum / cnt - c).astype(BF16)
    yc_ref[...] = (_dot(y, pw_ref[...]) * ps_ref[...]).astype(yc_ref.dtype)


def _in_mix(x, w, wcat, bmat, pw, ps, seq_len, d_a, d_c):
    t, d = x.shape
    n = w.shape[1]
    n_qkv = n - 2 * d_a - d_c
    tm = ROW_TILE
    full = lambda a: pl.BlockSpec(a.shape, lambda i: (0, 0))
    return pl.pallas_call(
        functools.partial(_in_mix_kernel, seq_len=seq_len),
        out_shape=(jax.ShapeDtypeStruct((t, n_qkv), BF16), jax.ShapeDtypeStruct((t, d_a), BF16),
                   jax.ShapeDtypeStruct((t, d_c), BF16)),
        grid=(t // tm,),
        in_specs=[pl.BlockSpec((tm, d), lambda i: (i, 0)), full(w), full(wcat), full(bmat), full(pw), full(ps)],
        out_specs=(pl.BlockSpec((tm, n_qkv), lambda i: (i, 0)),
                   pl.BlockSpec((tm, d_a), lambda i: (i, 0)),
                   pl.BlockSpec((tm, d_c), lambda i: (i, 0))),
        scratch_shapes=[pltpu.VMEM((POOL_HALO + tm, d_c), F32), pltpu.VMEM((POOL_HALO, d_c), F32)],
        compiler_params=_cparams(("arbitrary",), 48),
        name="in_proj_mix_ac",
    )(x, w, wcat, bmat, pw, ps)


def _attn_kernel(lam_ref, q_ref, k_ref, v_ref, bias_ref, g_ref, o_ref,
                 qs_ref, m_ref, l_ref, acc_ref, s0_ref, s1_ref, c0_ref, c1_ref):
    i = pl.program_id(2)
    tb, dv = q_ref.shape

    q = q_ref[...]
    lane = lax.broadcasted_iota(jnp.int32, (tb, dv), 1)
    qs_ref[0:tb, :] = jnp.where(lane < DH_B, q, jnp.zeros_like(q))
    qs_ref[tb:, :] = jnp.where(lane >= DH_B, q, jnp.zeros_like(q))
    m_ref[...] = jnp.full(m_ref.shape, -jnp.inf, F32)
    l_ref[...] = jnp.zeros(l_ref.shape, F32)
    acc_ref[...] = jnp.zeros(acc_ref.shape, F32)

    bufs = ((s0_ref, c0_ref), (s1_ref, c1_ref))

    def scores(buf, j, bias=None):
        s_ref, cm_ref = buf
        kj = k_ref[pl.ds(pl.multiple_of(j * tb, tb), tb), :]
        s = lax.dot_general(kj, qs_ref[...], (((1,), (1,)), ((), ())), preferred_element_type=F32)
        if bias is not None:
            s = s + jnp.concatenate([bias, bias], axis=1)
        s_ref[...] = s
        cm_ref[...] = jnp.max(s, axis=0, keepdims=True)

    def consume(buf, j, bias=None):
        s_ref, cm_ref = buf
        vjt = jnp.transpose(v_ref[pl.ds(pl.multiple_of(j * tb, tb), tb), :])
        s = s_ref[...]
        cm = cm_ref[...]
        if bias is not None:
            s = s + jnp.concatenate([bias, bias], axis=1)
            cm = jnp.max(s, axis=0, keepdims=True)
        m_prev = m_ref[...]
        m_next = jnp.maximum(m_prev, cm)
        alpha = jnp.exp2(m_prev - m_next)
        p = jnp.exp2(s - m_next)
        l_ref[...] = alpha * l_ref[...] + jnp.sum(p, axis=0, keepdims=True)
        m_ref[...] = m_next
        acc_ref[...] = alpha * acc_ref[...] + _dot(vjt, p.astype(BF16))

    def pipelined(base, n):
        for u in range(n):
            scores(bufs[(u + 1) % 2], base + u + 1)
            consume(bufs[u % 2], base + u)

    n_far = jnp.maximum(i - 1, 0)

    @pl.when(i == 0)
    def _():
        scores(bufs[0], 0, bias_ref[1])
        consume(bufs[0], 0)

    @pl.when(i == 1)
    def _():
        scores(bufs[0], 0, bias_ref[0])
        scores(bufs[1], 1, bias_ref[1])
        consume(bufs[0], 0)
        consume(bufs[1], 1)

    @pl.when(i >= 2)
    def _():
        scores(bufs[0], 0)

        def oct_body(t, carry):
            pipelined(8 * t, 8)
            return carry

        lax.fori_loop(0, n_far // 8, oct_body, 0)

        def quad_body(t, carry):
            pipelined((n_far // 8) * 8 + 4 * t, 4)
            return carry

        lax.fori_loop(0, (n_far % 8) // 4, quad_body, 0)

        def pair_body(t, carry):
            pipelined((n_far // 4) * 4 + 2 * t, 2)
            return carry

        lax.fori_loop(0, (n_far % 4) // 2, pair_body, 0)

        @pl.when(n_far % 2 == 0)
        def _():
            scores(bufs[1], i, bias_ref[1])
            consume(bufs[0], i - 1, bias_ref[0])
            consume(bufs[1], i)

        @pl.when(n_far % 2 == 1)
        def _():
            scores(bufs[1], i - 1, bias_ref[0])
            consume(bufs[0], i - 2)
            scores(bufs[0], i, bias_ref[1])
            consume(bufs[1], i - 1)
            consume(bufs[0], i)

    o = acc_ref[...] / l_ref[...]
    d = o[:, :tb] - lam_ref[0] * o[:, tb:]
    ms = jnp.mean(d * d, axis=0, keepdims=True)
    y = d * lax.rsqrt(ms + EPS) * g_ref[...]
    o_ref[...] = jnp.transpose(y).astype(o_ref.dtype)


def _attention(qkv, bias, lam, gain, batch, seq_len):
    t = qkv.shape[0]
    tb = ATTN_BLOCK
    nq = seq_len // tb
    dv = 2 * DH_B
    return pl.pallas_call(
        _attn_kernel,
        out_shape=jax.ShapeDtypeStruct((t, H_B * dv), BF16),
        grid=(batch, H_B, nq),
        in_specs=[pl.BlockSpec(memory_space=pltpu.SMEM),
                  pl.BlockSpec((tb, dv), lambda b, h, i: (b * nq + i, h)),
                  pl.BlockSpec((seq_len, dv), lambda b, h, i: (b, H_B + h)),
                  pl.BlockSpec((seq_len, dv), lambda b, h, i: (b, 2 * H_B + h)),
                  pl.BlockSpec((None, 2, tb, tb), lambda b, h, i: (h, 0, 0, 0)),
                  pl.BlockSpec((dv, 1), lambda b, h, i: (0, 0))],
        out_specs=pl.BlockSpec((tb, dv), lambda b, h, i: (b * nq + i, h)),
        scratch_shapes=[pltpu.VMEM((2 * tb, dv), BF16),
                        pltpu.VMEM((1, 2 * tb), F32),
                        pltpu.VMEM((1, 2 * tb), F32),
                        pltpu.VMEM((dv, 2 * tb), F32),
                        pltpu.VMEM((tb, 2 * tb), F32),
                        pltpu.VMEM((tb, 2 * tb), F32),
                        pltpu.VMEM((1, 2 * tb), F32),
                        pltpu.VMEM((1, 2 * tb), F32)],
        compiler_params=_cparams(("arbitrary", "arbitrary", "arbitrary"), 56),
        name="diff_attn",
    )(lam, qkv, qkv, qkv, bias, gain)


def _bias_kernel(rel_ref, o_ref):
    h = pl.program_id(0)
    tb = o_ref.shape[-1]
    kk = lax.broadcasted_iota(jnp.int32, (tb, tb), 0)
    qq = lax.broadcasted_iota(jnp.int32, (tb, tb), 1)
    far = rel_ref[(NUM_BUCKETS - 1) * H_B + h]
    for w, off in enumerate((tb, 0)):
        dist = off + qq - kk
        n = jnp.maximum(dist, 0)
        nf = jnp.maximum(n, 1).astype(F32)
        large = MAX_EXACT + (jnp.log(nf / MAX_EXACT) / math.log(MAX_DISTANCE / MAX_EXACT)
                             * (NUM_BUCKETS - MAX_EXACT)).astype(jnp.int32)
        bucket = jnp.where(n < MAX_EXACT, n, jnp.minimum(large, NUM_BUCKETS - 1))
        b = jnp.zeros((tb, tb), F32)
        for k in range(NUM_BUCKETS):
            b = jnp.where(bucket == k, rel_ref[k * H_B + h], b)
        o_ref[w] = jnp.where(dist >= 0, (b - far) * math.log2(math.e), -jnp.inf)


def _attn_bias_tiles(rel_bias, tb):
    return pl.pallas_call(
        _bias_kernel,
        out_shape=jax.ShapeDtypeStruct((H_B, 2, tb, tb), F32),
        grid=(H_B,),
        in_specs=[pl.BlockSpec(memory_space=pltpu.SMEM)],
        out_specs=pl.BlockSpec((None, 2, tb, tb), lambda h: (h, 0, 0, 0)),
        compiler_params=_cparams(("arbitrary",), 48),
        name="attn_bias",
    )(rel_bias.reshape(NUM_BUCKETS * H_B))


def _out_proj_kernel(x_ref, ya_ref, yb_ref, yc_ref, w_ref, g_ref, b_ref, o_ref, *, alpha):
    d_a = ya_ref.shape[1]
    d_b = yb_ref.shape[1]
    h = (_dot(ya_ref[...], w_ref[0:d_a, :]) + _dot(yb_ref[...], w_ref[d_a:d_a + d_b, :])
         + _dot(yc_ref[...], w_ref[d_a + d_b:, :]))
    o_ref[...] = _layer_norm(alpha * x_ref[...] + h, g_ref[...], b_ref[...])


def _out_proj(x, ya, yb, yc, w, g, b, alpha):
    t, d = x.shape
    tm = ROW_TILE
    row = lambda n: pl.BlockSpec((tm, n), lambda i: (i, 0))
    full = lambda a: pl.BlockSpec(a.shape, lambda i: (0, 0))
    return pl.pallas_call(
        functools.partial(_out_proj_kernel, alpha=alpha),
        out_shape=jax.ShapeDtypeStruct((t, d), F32),
        grid=(t // tm,),
        in_specs=[row(d), row(ya.shape[1]), row(yb.shape[1]), row(yc.shape[1]), full(w), full(g), full(b)],
        out_specs=row(d),
        compiler_params=_cparams(("arbitrary",), 48),
        name="out_proj_ln",
    )(x, ya, yb, yc, w, g, b)


def _swiglu(xb, w1, w3, w2):
    h1 = _dot(xb, w1)
    h3 = _dot(xb, w3)
    gate = h1 * (1.0 / (1.0 + jnp.exp(-h1)))
    return _dot((gate * h3).astype(BF16), w2)


def _ffn_kernel(x_ref, w1_ref, w3_ref, w2_ref, g_ref, b_ref, o_ref, *, alpha):
    x = x_ref[...]
    f = _swiglu(x.astype(BF16), w1_ref[...], w3_ref[...], w2_ref[...])
    o_ref[...] = _layer_norm(alpha * x + f, g_ref[...], b_ref[...])


def _ffn(x, w1, w3, w2, g, b, alpha):
    t, d = x.shape
    tm = ROW_TILE
    full = lambda a: pl.BlockSpec(a.shape, lambda i: (0, 0))
    return pl.pallas_call(
        functools.partial(_ffn_kernel, alpha=alpha),
        out_shape=jax.ShapeDtypeStruct((t, d), F32),
        grid=(t // tm,),
        in_specs=[pl.BlockSpec((tm, d), lambda i: (i, 0)), full(w1), full(w3), full(w2), full(g), full(b)],
        out_specs=pl.BlockSpec((tm, d), lambda i: (i, 0)),
        compiler_params=_cparams(("arbitrary",), 56),
        name="ffn_ln",
    )(x, w1, w3, w2, g, b)


def _route_tile(x, wh_ref, wl_ref, r_ref, cnt_ref, carry_ref):
    i = pl.program_id(0)
    tr = x.shape[0]
    ne = N_EXPERTS

    @pl.when(i == 0)
    def _():
        carry_ref[...] = jnp.zeros(carry_ref.shape, F32)

    xh, xl = _split_bf16(x)
    logits = _dot(xh, wh_ref[...]) + _dot(xl, wh_ref[...]) + _dot(xh, wl_ref[...])
    lt = jnp.transpose(logits)[0:ne, :]

    eio = lax.broadcasted_iota(jnp.int32, (ne, tr), 0)
    v1 = jnp.max(lt, axis=0, keepdims=True)
    i1 = jnp.min(jnp.where(lt == v1, eio, ne), axis=0, keepdims=True)
    oh1 = eio == i1
    lt2 = jnp.where(oh1, -jnp.inf, lt)
    v2 = jnp.max(lt2, axis=0, keepdims=True)
    i2 = jnp.min(jnp.where(lt2 == v2, eio, ne), axis=0, keepdims=True)
    oh2 = eio == i2
    e = jnp.exp(v2 - v1)
    g1 = 1.0 / (1.0 + e)
    g2 = e / (1.0 + e)

    member = jnp.where(oh1, 1.0, jnp.where(oh2, 1.0, 0.0))
    tri_r = lax.broadcasted_iota(jnp.int32, (tr, tr), 0)
    tri_c = lax.broadcasted_iota(jnp.int32, (tr, tr), 1)
    upper = jnp.where(tri_r < tri_c, 1.0, 0.0).astype(BF16)
    ranks = _dot(member.astype(BF16), upper) + carry_ref[:, 0:1]
    r1 = jnp.sum(jnp.where(oh1, ranks, 0.0), axis=0, keepdims=True)
    r2 = jnp.sum(jnp.where(oh2, ranks, 0.0), axis=0, keepdims=True)
    carry_ref[...] = carry_ref[...] + jnp.sum(member, axis=1, keepdims=True)
    cnt_ref[...] = carry_ref[...]

    rows = [i1.astype(F32), i2.astype(F32), g1, g2, r1, r2]
    out = jnp.zeros((ne, tr), F32)
    for k, row in enumerate(rows):
        out = jnp.where(eio == k, row, out)
    r_ref[...] = out


def _router_kernel(x_ref, wh_ref, wl_ref, r_ref, cnt_ref, carry_ref):
    _route_tile(x_ref[...], wh_ref, wl_ref, r_ref, cnt_ref, carry_ref)


def _router(x, router_w):
    t, d = x.shape
    tr = ROW_TILE
    rw = jnp.zeros((d, LANES), F32).at[:, :N_EXPERTS].set(router_w)
    wh = rw.astype(BF16)
    wl = (rw - wh.astype(F32)).astype(BF16)
    return pl.pallas_call(
        _router_kernel,
        out_shape=(jax.ShapeDtypeStruct((N_EXPERTS, t), F32), jax.ShapeDtypeStruct((N_EXPERTS, LANES), F32)),
        grid=(t // tr,),
        in_specs=[pl.BlockSpec((tr, d), lambda i: (i, 0)),
                  pl.BlockSpec(wh.shape, lambda i: (0, 0)),
                  pl.BlockSpec(wl.shape, lambda i: (0, 0))],
        out_specs=(pl.BlockSpec((N_EXPERTS, tr), lambda i: (0, i)),
                   pl.BlockSpec((N_EXPERTS, LANES), lambda i: (0, 0))),
        scratch_shapes=[pltpu.VMEM((N_EXPERTS, LANES), F32)],
        compiler_params=_cparams(("arbitrary",), 48),
        name="router",
    )(x, wh, wl)


def _row_copies(idx_vmem, idx_smem, idx_sem, n_lists, make_copy):
    cp = pltpu.make_async_copy(idx_vmem, idx_smem, idx_sem)
    cp.start()
    cp.wait()
    rows = idx_smem.shape[0]
    per_list = rows * LANES // n_lists
    for r in range(rows):
        for c in range(LANES):
            k, j = divmod(r * LANES + c, per_list)
            make_copy(k, j // SUBLANES, j % SUBLANES, idx_smem[r, c]).start(priority=c % 2)


def _dispatch_kernel(zpos_ref, zlen_ref, pos_ref, x_ref, xs_hbm, xbuf, zbuf, idx_ref, row_sem, idx_sem, zero_sem):
    i = pl.program_id(0)
    n = pl.num_programs(0)
    td = x_ref.shape[0]
    slot = i % 2
    zrows = zbuf.shape[0]

    def zero_region(r, act):
        start, length = zpos_ref[r], zlen_ref[r]
        n_big = length // zrows
        rest = start + n_big * zrows

        def big(q, carry):
            act(pltpu.make_async_copy(
                zbuf, xs_hbm.at[pl.ds(pl.multiple_of(start + q * zrows, SUBLANES), zrows)], zero_sem))
            return carry

        def small(q, carry):
            act(pltpu.make_async_copy(
                zbuf.at[0:SUBLANES], xs_hbm.at[pl.ds(pl.multiple_of(rest + q * SUBLANES, SUBLANES), SUBLANES)],
                zero_sem))
            return carry

        lax.fori_loop(0, n_big, big, 0)
        lax.fori_loop(0, (length - n_big * zrows) // SUBLANES, small, 0)

    @pl.when(i == 0)
    def _():
        zbuf[...] = jnp.zeros(zbuf.shape, F32)
        for r in range(zpos_ref.shape[0]):
            zero_region(r, lambda cp: cp.start())
        for r in range(zpos_ref.shape[0]):
            zero_region(r, lambda cp: cp.wait())

    def wait_slot(sl):
        for _ in range(2):
            pltpu.make_async_copy(xbuf.at[sl], xbuf.at[sl], row_sem.at[sl]).wait()

    @pl.when(i >= 2)
    def _():
        wait_slot(slot)

    for sl in range(2):
        @pl.when(slot == sl)
        def _(sl=sl):
            xbuf[sl] = x_ref[...].reshape(xbuf.shape[1:])
            _row_copies(pos_ref.at[0], idx_ref, idx_sem, 2,
                        lambda k, g, u, p: pltpu.make_async_copy(xbuf.at[sl, g, pl.ds(u, 1)],
                                                                 xs_hbm.at[pl.ds(p, 1)], row_sem.at[sl]))

    @pl.when(i == n - 1)
    def _():
        wait_slot(slot)

        @pl.when(n >= 2)
        def _():
            wait_slot(1 - slot)


def _dispatch(zpos, zlen, pos, x, n_rows):
    t, d = x.shape
    td = COMBINE_TILE
    grid_spec = pltpu.PrefetchScalarGridSpec(
        num_scalar_prefetch=2,
        grid=(t // td,),
        in_specs=[pl.BlockSpec((1, 2 * td // LANES, LANES), lambda i, zp, zl: (i, 0, 0)),
                  pl.BlockSpec((td, d), lambda i, zp, zl: (i, 0))],
        out_specs=pl.BlockSpec(memory_space=pl.ANY),
        scratch_shapes=[pltpu.VMEM((2, td // SUBLANES, SUBLANES, d), F32),
                        pltpu.VMEM((td, d), F32),
                        pltpu.SMEM((2 * td // LANES, LANES), jnp.int32),
                        pltpu.SemaphoreType.DMA((2,)),
                        pltpu.SemaphoreType.DMA(()),
                        pltpu.SemaphoreType.DMA(())],
    )
    return pl.pallas_call(
        _dispatch_kernel,
        out_shape=jax.ShapeDtypeStruct((n_rows, d), F32),
        grid_spec=grid_spec,
        compiler_params=_cparams(("arbitrary",), 48),
        name="moe_dispatch",
    )(zpos, zlen, pos, x)


def _moe_kernel(te_ref, na_ref, x_ref, w1_ref, w3_ref, w2_ref, ys_ref):
    active = pl.program_id(0) < na_ref[0]

    @pl.when(active)
    def _():
        ys_ref[...] = _swiglu(x_ref[...].astype(BF16), w1_ref[...], w3_ref[...], w2_ref[...])

    @pl.when(jnp.logical_not(active))
    def _():
        ys_ref[...] = jnp.zeros(ys_ref.shape, F32)


def _moe_experts(tile_expert, n_active, xs, n_tiles, tm, w1, w3, w2):
    d = xs.shape[1]
    f = w1.shape[2]
    grid_spec = pltpu.PrefetchScalarGridSpec(
        num_scalar_prefetch=2,
        grid=(n_tiles,),
        in_specs=[pl.BlockSpec((tm, d), lambda i, te, na: (jnp.minimum(i, na[0] - 1), 0)),
                  pl.BlockSpec((None, d, f), lambda i, te, na: (te[i], 0, 0)),
                  pl.BlockSpec((None, d, f), lambda i, te, na: (te[i], 0, 0)),
                  pl.BlockSpec((None, f, d), lambda i, te, na: (te[i], 0, 0))],
        out_specs=pl.BlockSpec((tm, d), lambda i, te, na: (i, 0)),
    )
    return pl.pallas_call(
        _moe_kernel,
        out_shape=jax.ShapeDtypeStruct((n_tiles * tm, d), F32),
        grid_spec=grid_spec,
        compiler_params=_cparams(("arbitrary",), 60),
        name="moe_experts",
    )(tile_expert, n_active, xs, w1, w3, w2)


def _combine_kernel(pos_ref, gate_ref, x_ref, ys_hbm, g_ref, b_ref, o_ref,
                    ybuf, idx_ref, row_sem, idx_sem, *, alpha):
    s = pl.program_id(0)
    n = pl.num_programs(0) - 1

    for sl in range(2):
        @pl.when(jnp.logical_and(s < n, s % 2 == sl))
        def _(sl=sl):
            _row_copies(pos_ref.at[0], idx_ref, idx_sem, 2,
                        lambda k, g, u, p: pltpu.make_async_copy(ys_hbm.at[pl.ds(p, 1)],
                                                                 ybuf.at[sl, k, g, pl.ds(u, 1)], row_sem.at[sl]))

    @pl.when(s > 0)
    def _():
        slot = (s - 1) % 2
        pltpu.make_async_copy(ybuf.at[slot], ybuf.at[slot], row_sem.at[slot]).wait()
        gates = gate_ref[...]
        f = (gates[:, 0:1] * ybuf[slot, 0].reshape(x_ref.shape)
             + gates[:, 1:2] * ybuf[slot, 1].reshape(x_ref.shape))
        o_ref[...] = _layer_norm(alpha * x_ref[...] + f, g_ref[...], b_ref[...])


def _combine(pos, gates, x, ys, g, b, alpha):
    t, d = x.shape
    tc = COMBINE_TILE
    n = t // tc
    cur = lambda s: (jnp.minimum(s, n - 1), 0, 0)
    done = lambda s: (jnp.maximum(s - 1, 0), 0)
    return pl.pallas_call(
        functools.partial(_combine_kernel, alpha=alpha),
        out_shape=jax.ShapeDtypeStruct((t, d), F32),
        grid=(n + 1,),
        in_specs=[pl.BlockSpec((1, 2 * tc // LANES, LANES), cur),
                  pl.BlockSpec((tc, 2), done),
                  pl.BlockSpec((tc, d), done),
                  pl.BlockSpec(memory_space=pl.ANY),
                  pl.BlockSpec(g.shape, lambda s: (0, 0)),
                  pl.BlockSpec(b.shape, lambda s: (0, 0))],
        out_specs=pl.BlockSpec((tc, d), done),
        scratch_shapes=[pltpu.VMEM((2, 2, tc // SUBLANES, SUBLANES, d), F32),
                        pltpu.SMEM((2 * tc // LANES, LANES), jnp.int32),
                        pltpu.SemaphoreType.DMA((2,)),
                        pltpu.SemaphoreType.DMA(())],
        compiler_params=_cparams(("arbitrary",), 48),
        name="moe_combine_ln",
    )(pos, gates, x, ys, g, b)


def _moe_ffn(x, router_w, w1, w3, w2, g, b, alpha):
    t, d = x.shape
    tm = MOE_TILE
    routed, counts = _router(x, router_w)

    e1 = routed[0].astype(jnp.int32)
    e2 = routed[1].astype(jnp.int32)
    r1 = routed[4].astype(jnp.int32)
    r2 = routed[5].astype(jnp.int32)
    cnt = counts[:, 0].astype(jnp.int32)
    tiles_per_expert = (cnt + tm - 1) // tm
    tile_end = jnp.cumsum(tiles_per_expert)
    row_start = (tile_end - tiles_per_expert) * tm
    pos1 = row_start[e1] + r1
    pos2 = row_start[e2] + r2
    n_tiles = (2 * t) // tm + N_EXPERTS
    n_active = tile_end[-1:]
    tile_ids = jnp.minimum(jnp.arange(n_tiles), n_active[0] - 1)
    tile_expert = jnp.minimum(jnp.sum(tile_ids[:, None] >= tile_end[None, :], axis=1),
                              N_EXPERTS - 1).astype(jnp.int32)
    tc = COMBINE_TILE
    pos = jnp.stack([pos1.reshape(t // tc, tc), pos2.reshape(t // tc, tc)], axis=1).reshape(
        t // tc, 2 * tc // LANES, LANES)
    used = tile_end[-1:] * tm
    zpos = jnp.concatenate([(row_start + cnt) // SUBLANES * SUBLANES, used]).astype(jnp.int32)
    zend = jnp.concatenate([tile_end * tm, jnp.full((1,), n_tiles * tm, jnp.int32)]).astype(jnp.int32)
    xs = _dispatch(zpos, zend - zpos, pos, x, n_tiles * tm)
    ys = _moe_experts(tile_expert, n_active.astype(jnp.int32), xs, n_tiles, tm, w1, w3, w2)

    gates = jnp.stack([routed[2], routed[3]], axis=1)
    return _combine(pos, gates, x, ys, g, b, alpha)


def kernel(x, w_in, w_out, gmlp_ws, gmlp_bs, lam_q1, lam_k1, lam_q2, lam_k2, diff_subln_g, rel_bias,
           pool_w, pool_scale, ln1_g, ln1_b, ln2_g, ln2_b, ffn_w1, ffn_w3, ffn_w2, router_w,
           moe_w1, moe_w3, moe_w2):
    batch, seq_len, d_model = x.shape
    depth = w_in.shape[0]
    d_a = w_out.shape[1] // 4
    d_b = H_B * 2 * DH_B
    d_c = pool_w.shape[1] * pool_w.shape[2]
    alpha = (2 * depth) ** 0.25
    t = batch * seq_len
    o_a = 2 * d_a
    o_q = o_a + d_b
    o_k = o_q + d_b
    o_v = o_k + d_b

    xt = x.reshape(t, d_model)
    bias = _attn_bias_tiles(rel_bias, ATTN_BLOCK)
    q_scale = DH_B ** -0.5 * math.log2(math.e)

    for l in range(depth):
        wl = w_in[l]
        w_cat = jnp.concatenate([wl[:, :o_a], wl[:, o_v:], wl[:, o_a:o_q] * q_scale, wl[:, o_q:o_v]],
                                axis=1).astype(BF16)
        wcat = jnp.transpose(gmlp_ws[l], (1, 0, 2)).reshape(CHUNK, H_A * CHUNK).astype(BF16)
        bmat = jnp.repeat(gmlp_bs[l].T, d_a // H_A, axis=1)
        pw = jax.scipy.linalg.block_diag(*[pool_w[l, gi] for gi in range(G_C)]).astype(BF16)
        ps = pool_scale[l].reshape(1, d_c)
        lam_init = 0.8 - 0.6 * math.exp(-0.3 * l)
        lam = (jnp.exp(jnp.sum(lam_q1[l] * lam_k1[l])) - jnp.exp(jnp.sum(lam_q2[l] * lam_k2[l]))
               + lam_init).reshape(1).astype(F32)
        gain = (diff_subln_g[l] * (1.0 - lam_init)).reshape(2 * DH_B, 1)

        qkv, ya, yc = _in_mix(xt, w_cat, wcat, bmat, pw, ps, seq_len, d_a, d_c)
        yb = _attention(qkv, bias, lam, gain, batch, seq_len)
        g1 = ln1_g[l].reshape(1, -1)
        b1 = ln1_b[l].reshape(1, -1)
        g2 = ln2_g[l].reshape(1, -1)
        b2 = ln2_b[l].reshape(1, -1)
        xt = _out_proj(xt, ya, yb, yc, w_out[l].astype(BF16), g1, b1, alpha)
        i = l // 2
        if l % 2 == 0:
            xt = _ffn(xt, ffn_w1[i].astype(BF16), ffn_w3[i].astype(BF16), ffn_w2[i].astype(BF16), g2, b2, alpha)
        else:
            xt = _moe_ffn(xt, router_w[i], moe_w1[i].astype(BF16), moe_w3[i].astype(BF16),
                          moe_w2[i].astype(BF16), g2, b2, alpha)
    return xt.reshape(batch, seq_len, d_model)
```

```python
import functools
import math

import jax
import jax.numpy as jnp
from jax import lax
from jax.experimental import pallas as pl
from jax.experimental.pallas import tpu as pltpu

F32 = jnp.float32
BF16 = jnp.bfloat16

H_A = 4
CHUNK = 128
H_B = 4
DH_B = 64
G_C = 4
POOL_WINDOWS = (2, 4, 8, 16)
POOL_HALO = 16
NUM_BUCKETS = 32
MAX_EXACT = NUM_BUCKETS // 2
MAX_DISTANCE = 128
N_EXPERTS = 8
EPS = 1e-5

LANES = 128
SUBLANES = 8
ONES_ROWS = 16
MIB = 1024 * 1024

ROW_TILE = 512
ATTN_BLOCK = 512
MOE_TILE = 512
COMBINE_TILE = 256


def _cparams(semantics, vmem_mib):
    return pltpu.CompilerParams(dimension_semantics=semantics, vmem_limit_bytes=vmem_mib * MIB)


def _dot(a, b):
    return jnp.dot(a, b, preferred_element_type=F32)


def _layer_norm(r, g, b):
    mu = jnp.mean(r, axis=-1, keepdims=True)
    d = r - mu
    var = jnp.mean(d * d, axis=-1, keepdims=True)
    return d * lax.rsqrt(var + EPS) * g + b


def _split_bf16(a):
    hi = a.astype(BF16)
    lo = (a - hi.astype(F32)).astype(BF16)
    return hi, lo


def _in_mix_kernel(x_ref, w_ref, wcat_ref, bmat_ref, pw_ref, ps_ref,
                   qkv_ref, ya_ref, yc_ref, ext_ref, halo_ref, *, seq_len):
    i = pl.program_id(0)
    tm = x_ref.shape[0]
    d_a = ya_ref.shape[1]
    d_c = yc_ref.shape[1]
    hd = d_a // H_A
    gd = d_c // G_C
    n_mix = 2 * d_a + d_c

    xb = x_ref[...].astype(BF16)
    pf = _dot(xb, w_ref[:, :n_mix])
    qkv_ref[...] = _dot(xb, w_ref[:, n_mix:]).astype(BF16)

    z = pf[:, :2 * d_a]
    g = 0.5 * z * (1.0 + lax.erf(z * (1.0 / math.sqrt(2.0))))
    u = g[:, :d_a]
    v = g[:, d_a:]
    ri = lax.broadcasted_iota(jnp.int32, (d_a, d_a), 0) // hd
    ci = lax.broadcasted_iota(jnp.int32, (d_a, d_a), 1) // hd
    grp = jnp.where(ri == ci, 1.0, 0.0).astype(BF16)

    def group_mean(a):
        hi, lo = _split_bf16(a)
        return (_dot(hi, grp) + _dot(lo, grp)) * (1.0 / hd)

    mu = group_mean(v)
    dv = v - mu
    var = group_mean(dv * dv)
    vn = dv * lax.rsqrt(var + EPS)

    wr = lax.broadcasted_iota(jnp.int32, wcat_ref.shape, 0)
    wc = lax.broadcasted_iota(jnp.int32, wcat_ref.shape, 1) % CHUNK
    wcat = jnp.where(wc <= wr, wcat_ref[...], jnp.zeros_like(wcat_ref[...]))
    lane_head = lax.broadcasted_iota(jnp.int32, (CHUNK, d_a), 1) // hd
    bmat = bmat_ref[...]
    for ck in range(tm // CHUNK):
        rows = slice(ck * CHUNK, (ck + 1) * CHUNK)
        vn_c = vn[rows]
        stacked = jnp.concatenate(
            [jnp.where(lane_head == h, vn_c, 0.0).astype(BF16) for h in range(H_A)], axis=0)
        sg = _dot(wcat, stacked) + bmat
        ya_ref[rows, :] = (u[rows] * sg).astype(ya_ref.dtype)

    c = pf[:, 2 * d_a:]
    pos0 = (i * tm) % seq_len

    @pl.when(pos0 == 0)
    def _():
        halo_ref[...] = jnp.zeros(halo_ref.shape, F32)

    ext_ref[0:POOL_HALO, :] = halo_ref[...]
    ext_ref[POOL_HALO:, :] = c
    halo_ref[...] = c[tm - POOL_HALO:, :]
    lane_grp = lax.broadcasted_iota(jnp.int32, (tm, d_c), 1) // gd
    run = c
    wsum = jnp.zeros_like(c)
    win = jnp.zeros((tm, d_c), jnp.int32)
    k = 1
    for gi, w in enumerate(POOL_WINDOWS):
        while k < w:
            run = run + ext_ref[pl.ds(POOL_HALO - k, tm), :]
            k += 1
        wsum = jnp.where(lane_grp == gi, run, wsum)
        win = jnp.where(lane_grp == gi, w, win)
    pos = pos0 + lax.broadcasted_iota(jnp.int32, (tm, d_c), 0)
    cnt = jnp.minimum(pos + 1, win).astype(F32)
    y = (wsum / cnt - c).astype(BF16)
    yc_ref[...] = (_dot(y, pw_ref[...]) * ps_ref[...]).astype(yc_ref.dtype)


def _in_mix(x, w, wcat, bmat, pw, ps, seq_len, d_a, d_c):
    t, d = x.shape
    n = w.shape[1]
    n_qkv = n - 2 * d_a - d_c
    tm = ROW_TILE
    full = lambda a: pl.BlockSpec(a.shape, lambda i: (0, 0))
    return pl.pallas_call(
        functools.partial(_in_mix_kernel, seq_len=seq_len),
        out_shape=(jax.ShapeDtypeStruct((t, n_qkv), BF16), jax.ShapeDtypeStruct((t, d_a), BF16),
                   jax.ShapeDtypeStruct((t, d_c), BF16)),
        grid=(t // tm,),
        in_specs=[pl.BlockSpec((tm, d), lambda i: (i, 0)), full(w), full(wcat), full(bmat), full(pw), full(ps)],
        out_specs=(pl.BlockSpec((tm, n_qkv), lambda i: (i, 0)),
                   pl.BlockSpec((tm, d_a), lambda i: (i, 0)),
                   pl.BlockSpec((tm, d_c), lambda i: (i, 0))),
        scratch_shapes=[pltpu.VMEM((POOL_HALO + tm, d_c), F32), pltpu.VMEM((POOL_HALO, d_c), F32)],
        compiler_params=_cparams(("arbitrary",), 48),
        name="in_proj_mix_ac",
    )(x, w, wcat, bmat, pw, ps)


def _attn_kernel(lam_ref, q_ref, k_ref, v_ref, bias_ref, g_ref, o_ref,
                 qs_ref, m_ref, l_ref, acc_ref, s0_ref, s1_ref, c0_ref, c1_ref):
    i = pl.program_id(2)
    tb, dv = q_ref.shape

    q = q_ref[...]
    lane = lax.broadcasted_iota(jnp.int32, (tb, dv), 1)
    qs_ref[0:tb, :] = jnp.where(lane < DH_B, q, jnp.zeros_like(q))
    qs_ref[tb:, :] = jnp.where(lane >= DH_B, q, jnp.zeros_like(q))
    m_ref[...] = jnp.full(m_ref.shape, -jnp.inf, F32)
    l_ref[...] = jnp.zeros(l_ref.shape, F32)
    acc_ref[...] = jnp.zeros(acc_ref.shape, F32)

    bufs = ((s0_ref, c0_ref), (s1_ref, c1_ref))

    def scores(buf, j, bias=None):
        s_ref, cm_ref = buf
        kj = k_ref[pl.ds(pl.multiple_of(j * tb, tb), tb), :]
        s = lax.dot_general(kj, qs_ref[...], (((1,), (1,)), ((), ())), preferred_element_type=F32)
        if bias is not None:
            s = s + jnp.concatenate([bias, bias], axis=1)
        s_ref[...] = s
        cm_ref[...] = jnp.max(s, axis=0, keepdims=True)

    def consume(buf, j, bias=None):
        s_ref, cm_ref = buf
        vjt = jnp.concatenate([jnp.transpose(v_ref[pl.ds(pl.multiple_of(j * tb, tb), tb), :]),
                               jnp.ones((ONES_ROWS, tb), BF16)], axis=0)
        s = s_ref[...]
        cm = cm_ref[...]
        if bias is not None:
            s = s + jnp.concatenate([bias, bias], axis=1)
            cm = jnp.max(s, axis=0, keepdims=True)
        m_prev = m_ref[...]
        m_next = jnp.maximum(m_prev, cm)
        alpha = jnp.exp2(m_prev - m_next)
        p = jnp.exp2(s - m_next)
        pv = _dot(vjt, p.astype(BF16))
        l_ref[...] = alpha * l_ref[...] + pv[dv:dv + 1]
        m_ref[...] = m_next
        acc_ref[...] = alpha * acc_ref[...] + pv[:dv]

    def pipelined(base, n):
        for u in range(n):
            scores(bufs[(u + 1) % 2], base + u + 1)
            consume(bufs[u % 2], base + u)

    n_far = jnp.maximum(i - 1, 0)

    @pl.when(i == 0)
    def _():
        scores(bufs[0], 0, bias_ref[1])
        consume(bufs[0], 0)

    @pl.when(i == 1)
    def _():
        scores(bufs[0], 0, bias_ref[0])
        scores(bufs[1], 1, bias_ref[1])
        consume(bufs[0], 0)
        consume(bufs[1], 1)

    @pl.when(i >= 2)
    def _():
        scores(bufs[0], 0)

        def oct_body(t, carry):
            pipelined(8 * t, 8)
            return carry

        lax.fori_loop(0, n_far // 8, oct_body, 0)

        def quad_body(t, carry):
            pipelined((n_far // 8) * 8 + 4 * t, 4)
            return carry

        lax.fori_loop(0, (n_far % 8) // 4, quad_body, 0)

        def pair_body(t, carry):
            pipelined((n_far // 4) * 4 + 2 * t, 2)
            return carry

        lax.fori_loop(0, (n_far % 4) // 2, pair_body, 0)

        @pl.when(n_far % 2 == 0)
        def _():
            scores(bufs[1], i, bias_ref[1])
            consume(bufs[0], i - 1, bias_ref[0])
            consume(bufs[1], i)

        @pl.when(n_far % 2 == 1)
        def _():
            scores(bufs[1], i - 1, bias_ref[0])
            consume(bufs[0], i - 2)
            scores(bufs[0], i, bias_ref[1])
            consume(bufs[1], i - 1)
            consume(bufs[0], i)

    o = acc_ref[...] / l_ref[...]
    d = o[:, :tb] - lam_ref[0] * o[:, tb:]
    ms = jnp.mean(d * d, axis=0, keepdims=True)
    y = d * lax.rsqrt(ms + EPS) * g_ref[...]
    o_ref[...] = jnp.transpose(y).astype(o_ref.dtype)


def _attention(qkv, bias, lam, gain, batch, seq_len):
    t = qkv.shape[0]
    tb = ATTN_BLOCK
    nq = seq_len // tb
    dv = 2 * DH_B
    return pl.pallas_call(
        _attn_kernel,
        out_shape=jax.ShapeDtypeStruct((t, H_B * dv), BF16),
        grid=(batch, H_B, nq),
        in_specs=[pl.BlockSpec(memory_space=pltpu.SMEM),
                  pl.BlockSpec((tb, dv), lambda b, h, i: (b * nq + i, h)),
                  pl.BlockSpec((seq_len, dv), lambda b, h, i: (b, H_B + h)),
                  pl.BlockSpec((seq_len, dv), lambda b, h, i: (b, 2 * H_B + h)),
                  pl.BlockSpec((None, 2, tb, tb), lambda b, h, i: (h, 0, 0, 0)),
                  pl.BlockSpec((dv, 1), lambda b, h, i: (0, 0))],
        out_specs=pl.BlockSpec((tb, dv), lambda b, h, i: (b * nq + i, h)),
        scratch_shapes=[pltpu.VMEM((2 * tb, dv), BF16),
                        pltpu.VMEM((1, 2 * tb), F32),
                        pltpu.VMEM((1, 2 * tb), F32),
                        pltpu.VMEM((dv, 2 * tb), F32),
                        pltpu.VMEM((tb, 2 * tb), F32),
                        pltpu.VMEM((tb, 2 * tb), F32),
                        pltpu.VMEM((1, 2 * tb), F32),
                        pltpu.VMEM((1, 2 * tb), F32)],
        compiler_params=_cparams(("arbitrary", "arbitrary", "arbitrary"), 56),
        name="diff_attn",
    )(lam, qkv, qkv, qkv, bias, gain)


def _bias_kernel(rel_ref, o_ref):
    h = pl.program_id(0)
    tb = o_ref.shape[-1]
    kk = lax.broadcasted_iota(jnp.int32, (tb, tb), 0)
    qq = lax.broadcasted_iota(jnp.int32, (tb, tb), 1)
    far = rel_ref[(NUM_BUCKETS - 1) * H_B + h]
    for w, off in enumerate((tb, 0)):
        dist = off + qq - kk
        n = jnp.maximum(dist, 0)
        nf = jnp.maximum(n, 1).astype(F32)
        large = MAX_EXACT + (jnp.log(nf / MAX_EXACT) / math.log(MAX_DISTANCE / MAX_EXACT)
                             * (NUM_BUCKETS - MAX_EXACT)).astype(jnp.int32)
        bucket = jnp.where(n < MAX_EXACT, n, jnp.minimum(large, NUM_BUCKETS - 1))
        b = jnp.zeros((tb, tb), F32)
        for k in range(NUM_BUCKETS):
            b = jnp.where(bucket == k, rel_ref[k * H_B + h], b)
        o_ref[w] = jnp.where(dist >= 0, (b - far) * math.log2(math.e), -jnp.inf)


def _attn_bias_tiles(rel_bias, tb):
    return pl.pallas_call(
        _bias_kernel,
        out_shape=jax.ShapeDtypeStruct((H_B, 2, tb, tb), F32),
        grid=(H_B,),
        in_specs=[pl.BlockSpec(memory_space=pltpu.SMEM)],
        out_specs=pl.BlockSpec((None, 2, tb, tb), lambda h: (h, 0, 0, 0)),
        compiler_params=_cparams(("arbitrary",), 48),
        name="attn_bias",
    )(rel_bias.reshape(NUM_BUCKETS * H_B))


def _out_proj_kernel(x_ref, ya_ref, yb_ref, yc_ref, w_ref, g_ref, b_ref, o_ref, *, alpha):
    d_a = ya_ref.shape[1]
    d_b = yb_ref.shape[1]
    h = (_dot(ya_ref[...], w_ref[0:d_a, :]) + _dot(yb_ref[...], w_ref[d_a:d_a + d_b, :])
         + _dot(yc_ref[...], w_ref[d_a + d_b:, :]))
    o_ref[...] = _layer_norm(alpha * x_ref[...] + h, g_ref[...], b_ref[...])


def _out_proj(x, ya, yb, yc, w, g, b, alpha):
    t, d = x.shape
    tm = ROW_TILE
    row = lambda n: pl.BlockSpec((tm, n), lambda i: (i, 0))
    full = lambda a: pl.BlockSpec(a.shape, lambda i: (0, 0))
    return pl.pallas_call(
        functools.partial(_out_proj_kernel, alpha=alpha),
        out_shape=jax.ShapeDtypeStruct((t, d), F32),
        grid=(t // tm,),
        in_specs=[row(d), row(ya.shape[1]), row(yb.shape[1]), row(yc.shape[1]), full(w), full(g), full(b)],
        out_specs=row(d),
        compiler_params=_cparams(("arbitrary",), 48),
        name="out_proj_ln",
    )(x, ya, yb, yc, w, g, b)


def _swiglu(xb, w1, w3, w2):
    h1 = _dot(xb, w1)
    h3 = _dot(xb, w3)
    gate = h1 * (1.0 / (1.0 + jnp.exp(-h1)))
    return _dot((gate * h3).astype(BF16), w2)


def _ffn_kernel(x_ref, w1_ref, w3_ref, w2_ref, g_ref, b_ref, o_ref, *, alpha):
    x = x_ref[...]
    f = _swiglu(x.astype(BF16), w1_ref[...], w3_ref[...], w2_ref[...])
    o_ref[...] = _layer_norm(alpha * x + f, g_ref[...], b_ref[...])


def _ffn(x, w1, w3, w2, g, b, alpha):
    t, d = x.shape
    tm = ROW_TILE
    full = lambda a: pl.BlockSpec(a.shape, lambda i: (0, 0))
    return pl.pallas_call(
        functools.partial(_ffn_kernel, alpha=alpha),
        out_shape=jax.ShapeDtypeStruct((t, d), F32),
        grid=(t // tm,),
        in_specs=[pl.BlockSpec((tm, d), lambda i: (i, 0)), full(w1), full(w3), full(w2), full(g), full(b)],
        out_specs=pl.BlockSpec((tm, d), lambda i: (i, 0)),
        compiler_params=_cparams(("arbitrary",), 56),
        name="ffn_ln",
    )(x, w1, w3, w2, g, b)


def _route_tile(x, wh_ref, wl_ref, r_ref, cnt_ref, carry_ref):
    i = pl.program_id(0)
    tr = x.shape[0]
    ne = N_EXPERTS

    @pl.when(i == 0)
    def _():
        carry_ref[...] = jnp.zeros(carry_ref.shape, F32)

    xh, xl = _split_bf16(x)
    logits = _dot(xh, wh_ref[...]) + _dot(xl, wh_ref[...]) + _dot(xh, wl_ref[...])
    lt = jnp.transpose(logits)[0:ne, :]

    eio = lax.broadcasted_iota(jnp.int32, (ne, tr), 0)
    v1 = jnp.max(lt, axis=0, keepdims=True)
    i1 = jnp.min(jnp.where(lt == v1, eio, ne), axis=0, keepdims=True)
    oh1 = eio == i1
    lt2 = jnp.where(oh1, -jnp.inf, lt)
    v2 = jnp.max(lt2, axis=0, keepdims=True)
    i2 = jnp.min(jnp.where(lt2 == v2, eio, ne), axis=0, keepdims=True)
    oh2 = eio == i2
    e = jnp.exp(v2 - v1)
    g1 = 1.0 / (1.0 + e)
    g2 = e / (1.0 + e)

    member = jnp.where(oh1, 1.0, jnp.where(oh2, 1.0, 0.0))
    tri_r = lax.broadcasted_iota(jnp.int32, (tr, tr), 0)
    tri_c = lax.broadcasted_iota(jnp.int32, (tr, tr), 1)
    upper = jnp.where(tri_r < tri_c, 1.0, 0.0).astype(BF16)
    ranks = _dot(member.astype(BF16), upper) + carry_ref[:, 0:1]
    r1 = jnp.sum(jnp.where(oh1, ranks, 0.0), axis=0, keepdims=True)
    r2 = jnp.sum(jnp.where(oh2, ranks, 0.0), axis=0, keepdims=True)
    carry_ref[...] = carry_ref[...] + jnp.sum(member, axis=1, keepdims=True)
    cnt_ref[...] = carry_ref[...]

    rows = [i1.astype(F32), i2.astype(F32), g1, g2, r1, r2]
    out = jnp.zeros((ne, tr), F32)
    for k, row in enumerate(rows):
        out = jnp.where(eio == k, row, out)
    r_ref[...] = out


def _router_kernel(x_ref, wh_ref, wl_ref, r_ref, cnt_ref, carry_ref):
    _route_tile(x_ref[...], wh_ref, wl_ref, r_ref, cnt_ref, carry_ref)


def _router(x, router_w):
    t, d = x.shape
    tr = ROW_TILE
    rw = jnp.zeros((d, LANES), F32).at[:, :N_EXPERTS].set(router_w)
    wh = rw.astype(BF16)
    wl = (rw - wh.astype(F32)).astype(BF16)
    return pl.pallas_call(
        _router_kernel,
        out_shape=(jax.ShapeDtypeStruct((N_EXPERTS, t), F32), jax.ShapeDtypeStruct((N_EXPERTS, LANES), F32)),
        grid=(t // tr,),
        in_specs=[pl.BlockSpec((tr, d), lambda i: (i, 0)),
                  pl.BlockSpec(wh.shape, lambda i: (0, 0)),
                  pl.BlockSpec(wl.shape, lambda i: (0, 0))],
        out_specs=(pl.BlockSpec((N_EXPERTS, tr), lambda i: (0, i)),
                   pl.BlockSpec((N_EXPERTS, LANES), lambda i: (0, 0))),
        scratch_shapes=[pltpu.VMEM((N_EXPERTS, LANES), F32)],
        compiler_params=_cparams(("arbitrary",), 48),
        name="router",
    )(x, wh, wl)


def _row_copies(idx_vmem, idx_smem, idx_sem, n_lists, make_copy):
    cp = pltpu.make_async_copy(idx_vmem, idx_smem, idx_sem)
    cp.start()
    cp.wait()
    rows = idx_smem.shape[0]
    per_list = rows * LANES // n_lists
    for r in range(rows):
        for c in range(LANES):
            k, j = divmod(r * LANES + c, per_list)
            make_copy(k, j // SUBLANES, j % SUBLANES, idx_smem[r, c]).start(priority=c % 2)


def _dispatch_kernel(zpos_ref, zlen_ref, pos_ref, x_ref, xs_hbm, xbuf, zbuf, idx_ref, row_sem, idx_sem, zero_sem):
    i = pl.program_id(0)
    n = pl.num_programs(0)
    td = x_ref.shape[0]
    slot = i % 2
    zrows = zbuf.shape[0]

    def zero_region(r, act):
        start, length = zpos_ref[r], zlen_ref[r]
        n_big = length // zrows
        rest = start + n_big * zrows

        def big(q, carry):
            act(pltpu.make_async_copy(
                zbuf, xs_hbm.at[pl.ds(pl.multiple_of(start + q * zrows, SUBLANES), zrows)], zero_sem))
            return carry

        def small(q, carry):
            act(pltpu.make_async_copy(
                zbuf.at[0:SUBLANES], xs_hbm.at[pl.ds(pl.multiple_of(rest + q * SUBLANES, SUBLANES), SUBLANES)],
                zero_sem))
            return carry

        lax.fori_loop(0, n_big, big, 0)
        lax.fori_loop(0, (length - n_big * zrows) // SUBLANES, small, 0)

    @pl.when(i == 0)
    def _():
        zbuf[...] = jnp.zeros(zbuf.shape, F32)
        for r in range(zpos_ref.shape[0]):
            zero_region(r, lambda cp: cp.start())
        for r in range(zpos_ref.shape[0]):
            zero_region(r, lambda cp: cp.wait())

    def wait_slot(sl):
        for _ in range(2):
            pltpu.make_async_copy(xbuf.at[sl], xbuf.at[sl], row_sem.at[sl]).wait()

    @pl.when(i >= 2)
    def _():
        wait_slot(slot)

    for sl in range(2):
        @pl.when(slot == sl)
        def _(sl=sl):
            xbuf[sl] = x_ref[...].reshape(xbuf.shape[1:])
            _row_copies(pos_ref.at[0], idx_ref, idx_sem, 2,
                        lambda k, g, u, p: pltpu.make_async_copy(xbuf.at[sl, g, pl.ds(u, 1)],
                                                                 xs_hbm.at[pl.ds(p, 1)], row_sem.at[sl]))

    @pl.when(i == n - 1)
    def _():
        wait_slot(slot)

        @pl.when(n >= 2)
        def _():
            wait_slot(1 - slot)


def _dispatch(zpos, zlen, pos, x, n_rows):
    t, d = x.shape
    td = COMBINE_TILE
    grid_spec = pltpu.PrefetchScalarGridSpec(
        num_scalar_prefetch=2,
        grid=(t // td,),
        in_specs=[pl.BlockSpec((1, 2 * td // LANES, LANES), lambda i, zp, zl: (i, 0, 0)),
                  pl.BlockSpec((td, d), lambda i, zp, zl: (i, 0))],
        out_specs=pl.BlockSpec(memory_space=pl.ANY),
        scratch_shapes=[pltpu.VMEM((2, td // SUBLANES, SUBLANES, d), F32),
                        pltpu.VMEM((td, d), F32),
                        pltpu.SMEM((2 * td // LANES, LANES), jnp.int32),
                        pltpu.SemaphoreType.DMA((2,)),
                        pltpu.SemaphoreType.DMA(()),
                        pltpu.SemaphoreType.DMA(())],
    )
    return pl.pallas_call(
        _dispatch_kernel,
        out_shape=jax.ShapeDtypeStruct((n_rows, d), F32),
        grid_spec=grid_spec,
        compiler_params=_cparams(("arbitrary",), 48),
        name="moe_dispatch",
    )(zpos, zlen, pos, x)


def _moe_kernel(te_ref, na_ref, x_ref, w1_ref, w3_ref, w2_ref, ys_ref):
    active = pl.program_id(0) < na_ref[0]

    @pl.when(active)
    def _():
        ys_ref[...] = _swiglu(x_ref[...].astype(BF16), w1_ref[...], w3_ref[...], w2_ref[...])

    @pl.when(jnp.logical_not(active))
    def _():
        ys_ref[...] = jnp.zeros(ys_ref.shape, F32)


def _moe_experts(tile_expert, n_active, xs, n_tiles, tm, w1, w3, w2):
    d = xs.shape[1]
    f = w1.shape[2]
    grid_spec = pltpu.PrefetchScalarGridSpec(
        num_scalar_prefetch=2,
        grid=(n_tiles,),
        in_specs=[pl.BlockSpec((tm, d), lambda i, te, na: (jnp.minimum(i, na[0] - 1), 0)),
                  pl.BlockSpec((None, d, f), lambda i, te, na: (te[i], 0, 0)),
                  pl.BlockSpec((None, d, f), lambda i, te, na: (te[i], 0, 0)),
                  pl.BlockSpec((None, f, d), lambda i, te, na: (te[i], 0, 0))],
        out_specs=pl.BlockSpec((tm, d), lambda i, te, na: (i, 0)),
    )
    return pl.pallas_call(
        _moe_kernel,
        out_shape=jax.ShapeDtypeStruct((n_tiles * tm, d), F32),
        grid_spec=grid_spec,
        compiler_params=_cparams(("arbitrary",), 60),
        name="moe_experts",
    )(tile_expert, n_active, xs, w1, w3, w2)


def _combine_kernel(pos_ref, gate_ref, x_ref, ys_hbm, g_ref, b_ref, o_ref,
                    ybuf, idx_ref, row_sem, idx_sem, *, alpha):
    s = pl.program_id(0)
    n = pl.num_programs(0) - 1

    for sl in range(2):
        @pl.when(jnp.logical_and(s < n, s % 2 == sl))
        def _(sl=sl):
            _row_copies(pos_ref.at[0], idx_ref, idx_sem, 2,
                        lambda k, g, u, p: pltpu.make_async_copy(ys_hbm.at[pl.ds(p, 1)],
                                                                 ybuf.at[sl, k, g, pl.ds(u, 1)], row_sem.at[sl]))

    @pl.when(s > 0)
    def _():
        slot = (s - 1) % 2
        pltpu.make_async_copy(ybuf.at[slot], ybuf.at[slot], row_sem.at[slot]).wait()
        gates = gate_ref[...]
        f = (gates[:, 0:1] * ybuf[slot, 0].reshape(x_ref.shape)
             + gates[:, 1:2] * ybuf[slot, 1].reshape(x_ref.shape))
        o_ref[...] = _layer_norm(alpha * x_ref[...] + f, g_ref[...], b_ref[...])


def _combine(pos, gates, x, ys, g, b, alpha):
    t, d = x.shape
    tc = COMBINE_TILE
    n = t // tc
    cur = lambda s: (jnp.minimum(s, n - 1), 0, 0)
    done = lambda s: (jnp.maximum(s - 1, 0), 0)
    return pl.pallas_call(
        functools.partial(_combine_kernel, alpha=alpha),
        out_shape=jax.ShapeDtypeStruct((t, d), F32),
        grid=(n + 1,),
        in_specs=[pl.BlockSpec((1, 2 * tc // LANES, LANES), cur),
                  pl.BlockSpec((tc, 2), done),
                  pl.BlockSpec((tc, d), done),
                  pl.BlockSpec(memory_space=pl.ANY),
                  pl.BlockSpec(g.shape, lambda s: (0, 0)),
                  pl.BlockSpec(b.shape, lambda s: (0, 0))],
        out_specs=pl.BlockSpec((tc, d), done),
        scratch_shapes=[pltpu.VMEM((2, 2, tc // SUBLANES, SUBLANES, d), F32),
                        pltpu.SMEM((2 * tc // LANES, LANES), jnp.int32),
                        pltpu.SemaphoreType.DMA((2,)),
                        pltpu.SemaphoreType.DMA(())],
        compiler_params=_cparams(("arbitrary",), 48),
        name="moe_combine_ln",
    )(pos, gates, x, ys, g, b)


def _moe_ffn(x, router_w, w1, w3, w2, g, b, alpha):
    t, d = x.shape
    tm = MOE_TILE
    routed, counts = _router(x, router_w)

    e1 = routed[0].astype(jnp.int32)
    e2 = routed[1].astype(jnp.int32)
    r1 = routed[4].astype(jnp.int32)
    r2 = routed[5].astype(jnp.int32)
    cnt = counts[:, 0].astype(jnp.int32)
    tiles_per_expert = (cnt + tm - 1) // tm
    tile_end = jnp.cumsum(tiles_per_expert)
    row_start = (tile_end - tiles_per_expert) * tm
    pos1 = row_start[e1] + r1
    pos2 = row_start[e2] + r2
    n_tiles = (2 * t) // tm + N_EXPERTS
    n_active = tile_end[-1:]
    tile_ids = jnp.minimum(jnp.arange(n_tiles), n_active[0] - 1)
    tile_expert = jnp.minimum(jnp.sum(tile_ids[:, None] >= tile_end[None, :], axis=1),
                              N_EXPERTS - 1).astype(jnp.int32)
    tc = COMBINE_TILE
    pos = jnp.stack([pos1.reshape(t // tc, tc), pos2.reshape(t // tc, tc)], axis=1).reshape(
        t // tc, 2 * tc // LANES, LANES)
    used = tile_end[-1:] * tm
    zpos = jnp.concatenate([(row_start + cnt) // SUBLANES * SUBLANES, used]).astype(jnp.int32)
    zend = jnp.concatenate([tile_end * tm, jnp.full((1,), n_tiles * tm, jnp.int32)]).astype(jnp.int32)
    xs = _dispatch(zpos, zend - zpos, pos, x, n_tiles * tm)
    ys = _moe_experts(tile_expert, n_active.astype(jnp.int32), xs, n_tiles, tm, w1, w3, w2)

    gates = jnp.stack([routed[2], routed[3]], axis=1)
    return _combine(pos, gates, x, ys, g, b, alpha)


def kernel(x, w_in, w_out, gmlp_ws, gmlp_bs, lam_q1, lam_k1, lam_q2, lam_k2, diff_subln_g, rel_bias,
           pool_w, pool_scale, ln1_g, ln1_b, ln2_g, ln2_b, ffn_w1, ffn_w3, ffn_w2, router_w,
           moe_w1, moe_w3, moe_w2):
    batch, seq_len, d_model = x.shape
    depth = w_in.shape[0]
    d_a = w_out.shape[1] // 4
    d_b = H_B * 2 * DH_B
    d_c = pool_w.shape[1] * pool_w.shape[2]
    alpha = (2 * depth) ** 0.25
    t = batch * seq_len
    o_a = 2 * d_a
    o_q = o_a + d_b
    o_k = o_q + d_b
    o_v = o_k + d_b

    xt = x.reshape(t, d_model)
    bias = _attn_bias_tiles(rel_bias, ATTN_BLOCK)
    q_scale = DH_B ** -0.5 * math.log2(math.e)

    for l in range(depth):
        wl = w_in[l]
        w_cat = jnp.concatenate([wl[:, :o_a], wl[:, o_v:], wl[:, o_a:o_q] * q_scale, wl[:, o_q:o_v]],
                                axis=1).astype(BF16)
        wcat = jnp.transpose(gmlp_ws[l], (1, 0, 2)).reshape(CHUNK, H_A * CHUNK).astype(BF16)
        bmat = jnp.repeat(gmlp_bs[l].T, d_a // H_A, axis=1)
        pw = jax.scipy.linalg.block_diag(*[pool_w[l, gi] for gi in range(G_C)]).astype(BF16)
        ps = pool_scale[l].reshape(1, d_c)
        lam_init = 0.8 - 0.6 * math.exp(-0.3 * l)
        lam = (jnp.exp(jnp.sum(lam_q1[l] * lam_k1[l])) - jnp.exp(jnp.sum(lam_q2[l] * lam_k2[l]))
               + lam_init).reshape(1).astype(F32)
        gain = (diff_subln_g[l] * (1.0 - lam_init)).reshape(2 * DH_B, 1)

        qkv, ya, yc = _in_mix(xt, w_cat, wcat, bmat, pw, ps, seq_len, d_a, d_c)
        yb = _attention(qkv, bias, lam, gain, batch, seq_len)
        g1 = ln1_g[l].reshape(1, -1)
        b1 = ln1_b[l].reshape(1, -1)
        g2 = ln2_g[l].reshape(1, -1)
        b2 = ln2_b[l].reshape(1, -1)
        xt = _out_proj(xt, ya, yb, yc, w_out[l].astype(BF16), g1, b1, alpha)
        i = l // 2
        if l % 2 == 0:
            xt = _ffn(xt, ffn_w1[i].astype(BF16), ffn_w3[i].astype(BF16), ffn_w2[i].astype(BF16), g2, b2, alpha)
        else:
            xt = _moe_ffn(xt, router_w[i], moe_w1[i].astype(BF16), moe_w3[i].astype(BF16),
                          moe_w2[i].astype(BF16), g2, b2, alpha)
    return xt.reshape(batch, seq_len, d_model)
```

```python
import functools
import math

import jax
import jax.numpy as jnp
from jax import lax
from jax.experimental import pallas as pl
from jax.experimental.pallas import tpu as pltpu

F32 = jnp.float32
BF16 = jnp.bfloat16

H_A = 4
CHUNK = 128
H_B = 4
DH_B = 64
G_C = 4
POOL_WINDOWS = (2, 4, 8, 16)
POOL_HALO = 16
NUM_BUCKETS = 32
MAX_EXACT = NUM_BUCKETS // 2
MAX_DISTANCE = 128
N_EXPERTS = 8
EPS = 1e-5

LANES = 128
SUBLANES = 8
ONES_ROWS = 16
MIB = 1024 * 1024
VMEM_BYTES = 64 * MIB
VMEM_LIMIT_SMALL = 3 * VMEM_BYTES // 4
VMEM_LIMIT_LARGE = 7 * VMEM_BYTES // 8
VMEM_LIMIT_EXPERTS = 15 * VMEM_BYTES // 16

ROW_TILE = 512
ATTN_BLOCK = 512
MOE_TILE = 512
COMBINE_TILE = 512


def _cparams(semantics, vmem_limit):
    return pltpu.CompilerParams(dimension_semantics=semantics, vmem_limit_bytes=vmem_limit)


def _dot(a, b):
    return jnp.dot(a, b, preferred_element_type=F32)


def _layer_norm(r, g, b):
    mu = jnp.mean(r, axis=-1, keepdims=True)
    d = r - mu
    var = jnp.mean(d * d, axis=-1, keepdims=True)
    return d * lax.rsqrt(var + EPS) * g + b


def _split_bf16(a):
    hi = a.astype(BF16)
    lo = (a - hi.astype(F32)).astype(BF16)
    return hi, lo


def _in_mix_kernel(x_ref, w_ref, wcat_ref, bmat_ref, pw_ref, ps_ref,
                   qkv_ref, ya_ref, yc_ref, ext_ref, halo_ref, *, seq_len):
    i = pl.program_id(0)
    tm = x_ref.shape[0]
    d_a = ya_ref.shape[1]
    d_c = yc_ref.shape[1]
    hd = d_a // H_A
    gd = d_c // G_C
    n_mix = 2 * d_a + d_c

    xb = x_ref[...].astype(BF16)
    pf = _dot(xb, w_ref[:, :n_mix])
    qkv_ref[...] = _dot(xb, w_ref[:, n_mix:]).astype(BF16)

    z = pf[:, :2 * d_a]
    g = 0.5 * z * (1.0 + lax.erf(z * (1.0 / math.sqrt(2.0))))
    u = g[:, :d_a]
    v = g[:, d_a:]
    ri = lax.broadcasted_iota(jnp.int32, (d_a, d_a), 0) // hd
    ci = lax.broadcasted_iota(jnp.int32, (d_a, d_a), 1) // hd
    grp = jnp.where(ri == ci, 1.0, 0.0).astype(BF16)

    def group_mean(a):
        hi, lo = _split_bf16(a)
        return (_dot(hi, grp) + _dot(lo, grp)) * (1.0 / hd)

    mu = group_mean(v)
    dv = v - mu
    var = group_mean(dv * dv)
    vn = dv * lax.rsqrt(var + EPS)

    wr = lax.broadcasted_iota(jnp.int32, wcat_ref.shape, 0)
    wc = lax.broadcasted_iota(jnp.int32, wcat_ref.shape, 1) % CHUNK
    wcat = jnp.where(wc <= wr, wcat_ref[...], jnp.zeros_like(wcat_ref[...]))
    lane_head = lax.broadcasted_iota(jnp.int32, (CHUNK, d_a), 1) // hd
    bmat = bmat_ref[...]
    for ck in range(tm // CHUNK):
        rows = slice(ck * CHUNK, (ck + 1) * CHUNK)
        vn_c = vn[rows]
        stacked = jnp.concatenate(
            [jnp.where(lane_head == h, vn_c, 0.0).astype(BF16) for h in range(H_A)], axis=0)
        sg = _dot(wcat, stacked) + bmat
        ya_ref[rows, :] = (u[rows] * sg).astype(ya_ref.dtype)

    c = pf[:, 2 * d_a:]
    pos0 = (i * tm) % seq_len

    @pl.when(pos0 == 0)
    def _():
        halo_ref[...] = jnp.zeros(halo_ref.shape, F32)

    ext_ref[0:POOL_HALO, :] = halo_ref[...]
    ext_ref[POOL_HALO:, :] = c
    halo_ref[...] = c[tm - POOL_HALO:, :]
    lane_grp = lax.broadcasted_iota(jnp.int32, (tm, d_c), 1) // gd
    run = c
    wsum = jnp.zeros_like(c)
    win = jnp.zeros((tm, d_c), jnp.int32)
    k = 1
    for gi, w in enumerate(POOL_WINDOWS):
        while k < w:
            run = run + ext_ref[pl.ds(POOL_HALO - k, tm), :]
            k += 1
        wsum = jnp.where(lane_grp == gi, run, wsum)
        win = jnp.where(lane_grp == gi, w, win)
    pos = pos0 + lax.broadcasted_iota(jnp.int32, (tm, d_c), 0)
    cnt = jnp.minimum(pos + 1, win).astype(F32)
    y = (wsum / cnt - c).astype(BF16)
    yc_ref[...] = (_dot(y, pw_ref[...]) * ps_ref[...]).astype(yc_ref.dtype)


def _in_mix(x, w, wcat, bmat, pw, ps, seq_len, d_a, d_c):
    t, d = x.shape
    n = w.shape[1]
    n_qkv = n - 2 * d_a - d_c
    tm = ROW_TILE
    full = lambda a: pl.BlockSpec(a.shape, lambda i: (0, 0))
    return pl.pallas_call(
        functools.partial(_in_mix_kernel, seq_len=seq_len),
        out_shape=(jax.ShapeDtypeStruct((t, n_qkv), BF16), jax.ShapeDtypeStruct((t, d_a), BF16),
                   jax.ShapeDtypeStruct((t, d_c), BF16)),
        grid=(t // tm,),
        in_specs=[pl.BlockSpec((tm, d), lambda i: (i, 0)), full(w), full(wcat), full(bmat), full(pw), full(ps)],
        out_specs=(pl.BlockSpec((tm, n_qkv), lambda i: (i, 0)),
                   pl.BlockSpec((tm, d_a), lambda i: (i, 0)),
                   pl.BlockSpec((tm, d_c), lambda i: (i, 0))),
        scratch_shapes=[pltpu.VMEM((POOL_HALO + tm, d_c), F32), pltpu.VMEM((POOL_HALO, d_c), F32)],
        compiler_params=_cparams(("arbitrary",), VMEM_LIMIT_SMALL),
        name="in_proj_mix_ac",
    )(x, w, wcat, bmat, pw, ps)


def _attn_kernel(lam_ref, q_ref, k_ref, v_ref, bias_ref, g_ref, o_ref,
                 qs_ref, m_ref, l_ref, acc_ref, s0_ref, s1_ref, c0_ref, c1_ref):
    i = pl.program_id(2)
    tb, dv = q_ref.shape

    q = q_ref[...]
    lane = lax.broadcasted_iota(jnp.int32, (tb, dv), 1)
    qs_ref[0:tb, :] = jnp.where(lane < DH_B, q, jnp.zeros_like(q))
    qs_ref[tb:, :] = jnp.where(lane >= DH_B, q, jnp.zeros_like(q))
    m_ref[...] = jnp.full(m_ref.shape, -jnp.inf, F32)
    l_ref[...] = jnp.zeros(l_ref.shape, F32)
    acc_ref[...] = jnp.zeros(acc_ref.shape, F32)

    bufs = ((s0_ref, c0_ref), (s1_ref, c1_ref))

    def scores(buf, j, bias=None):
        s_ref, cm_ref = buf
        kj = k_ref[pl.ds(pl.multiple_of(j * tb, tb), tb), :]
        s = lax.dot_general(kj, qs_ref[...], (((1,), (1,)), ((), ())), preferred_element_type=F32)
        if bias is not None:
            s = s + jnp.concatenate([bias, bias], axis=1)
        s_ref[...] = s
        cm_ref[...] = jnp.max(s, axis=0, keepdims=True)

    def consume(buf, j, bias=None):
        s_ref, cm_ref = buf
        vjt = jnp.concatenate([jnp.transpose(v_ref[pl.ds(pl.multiple_of(j * tb, tb), tb), :]),
                               jnp.ones((ONES_ROWS, tb), BF16)], axis=0)
        s = s_ref[...]
        cm = cm_ref[...]
        if bias is not None:
            s = s + jnp.concatenate([bias, bias], axis=1)
            cm = jnp.max(s, axis=0, keepdims=True)
        m_prev = m_ref[...]
        m_next = jnp.maximum(m_prev, cm)
        alpha = jnp.exp2(m_prev - m_next)
        p = jnp.exp2(s - m_next)
        pv = _dot(vjt, p.astype(BF16))
        l_ref[...] = alpha * l_ref[...] + pv[dv:dv + 1]
        m_ref[...] = m_next
        acc_ref[...] = alpha * acc_ref[...] + pv[:dv]

    def pipelined(base, n):
        for u in range(n):
            scores(bufs[(u + 1) % 2], base + u + 1)
            consume(bufs[u % 2], base + u)

    n_far = jnp.maximum(i - 1, 0)

    @pl.when(i == 0)
    def _():
        scores(bufs[0], 0, bias_ref[1])
        consume(bufs[0], 0)

    @pl.when(i == 1)
    def _():
        scores(bufs[0], 0, bias_ref[0])
        scores(bufs[1], 1, bias_ref[1])
        consume(bufs[0], 0)
        consume(bufs[1], 1)

    @pl.when(i >= 2)
    def _():
        scores(bufs[0], 0)

        def oct_body(t, carry):
            pipelined(8 * t, 8)
            return carry

        lax.fori_loop(0, n_far // 8, oct_body, 0)

        def quad_body(t, carry):
            pipelined((n_far // 8) * 8 + 4 * t, 4)
            return carry

        lax.fori_loop(0, (n_far % 8) // 4, quad_body, 0)

        def pair_body(t, carry):
            pipelined((n_far // 4) * 4 + 2 * t, 2)
            return carry

        lax.fori_loop(0, (n_far % 4) // 2, pair_body, 0)

        @pl.when(n_far % 2 == 0)
        def _():
            scores(bufs[1], i, bias_ref[1])
            consume(bufs[0], i - 1, bias_ref[0])
            consume(bufs[1], i)

        @pl.when(n_far % 2 == 1)
        def _():
            scores(bufs[1], i - 1, bias_ref[0])
            consume(bufs[0], i - 2)
            scores(bufs[0], i, bias_ref[1])
            consume(bufs[1], i - 1)
            consume(bufs[0], i)

    o = acc_ref[...] / l_ref[...]
    d = o[:, :tb] - lam_ref[0] * o[:, tb:]
    ms = jnp.mean(d * d, axis=0, keepdims=True)
    y = d * lax.rsqrt(ms + EPS) * g_ref[...]
    o_ref[...] = jnp.transpose(y).astype(o_ref.dtype)


def _attention(qkv, bias, lam, gain, batch, seq_len):
    t = qkv.shape[0]
    tb = ATTN_BLOCK
    nq = seq_len // tb
    dv = 2 * DH_B
    return pl.pallas_call(
        _attn_kernel,
        out_shape=jax.ShapeDtypeStruct((t, H_B * dv), BF16),
        grid=(batch, H_B, nq),
        in_specs=[pl.BlockSpec(memory_space=pltpu.SMEM),
                  pl.BlockSpec((tb, dv), lambda b, h, i: (b * nq + i, h)),
                  pl.BlockSpec((seq_len, dv), lambda b, h, i: (b, H_B + h)),
                  pl.BlockSpec((seq_len, dv), lambda b, h, i: (b, 2 * H_B + h)),
                  pl.BlockSpec((None, 2, tb, tb), lambda b, h, i: (h, 0, 0, 0)),
                  pl.BlockSpec((dv, 1), lambda b, h, i: (0, 0))],
        out_specs=pl.BlockSpec((tb, dv), lambda b, h, i: (b * nq + i, h)),
        scratch_shapes=[pltpu.VMEM((2 * tb, dv), BF16),
                        pltpu.VMEM((1, 2 * tb), F32),
                        pltpu.VMEM((1, 2 * tb), F32),
                        pltpu.VMEM((dv, 2 * tb), F32),
                        pltpu.VMEM((tb, 2 * tb), F32),
                        pltpu.VMEM((tb, 2 * tb), F32),
                        pltpu.VMEM((1, 2 * tb), F32),
                        pltpu.VMEM((1, 2 * tb), F32)],
        compiler_params=_cparams(("arbitrary", "arbitrary", "arbitrary"), VMEM_LIMIT_LARGE),
        name="diff_attn",
    )(lam, qkv, qkv, qkv, bias, gain)


def _bias_kernel(rel_ref, o_ref):
    h = pl.program_id(0)
    tb = o_ref.shape[-1]
    kk = lax.broadcasted_iota(jnp.int32, (tb, tb), 0)
    qq = lax.broadcasted_iota(jnp.int32, (tb, tb), 1)
    far = rel_ref[(NUM_BUCKETS - 1) * H_B + h]
    for w, off in enumerate((tb, 0)):
        dist = off + qq - kk
        n = jnp.maximum(dist, 0)
        nf = jnp.maximum(n, 1).astype(F32)
        large = MAX_EXACT + (jnp.log(nf / MAX_EXACT) / math.log(MAX_DISTANCE / MAX_EXACT)
                             * (NUM_BUCKETS - MAX_EXACT)).astype(jnp.int32)
        bucket = jnp.where(n < MAX_EXACT, n, jnp.minimum(large, NUM_BUCKETS - 1))
        b = jnp.zeros((tb, tb), F32)
        for k in range(NUM_BUCKETS):
            b = jnp.where(bucket == k, rel_ref[k * H_B + h], b)
        o_ref[w] = jnp.where(dist >= 0, (b - far) * math.log2(math.e), -jnp.inf)


def _attn_bias_tiles(rel_bias, tb):
    return pl.pallas_call(
        _bias_kernel,
        out_shape=jax.ShapeDtypeStruct((H_B, 2, tb, tb), F32),
        grid=(H_B,),
        in_specs=[pl.BlockSpec(memory_space=pltpu.SMEM)],
        out_specs=pl.BlockSpec((None, 2, tb, tb), lambda h: (h, 0, 0, 0)),
        compiler_params=_cparams(("arbitrary",), VMEM_LIMIT_SMALL),
        name="attn_bias",
    )(rel_bias.reshape(NUM_BUCKETS * H_B))


def _out_proj_kernel(x_ref, ya_ref, yb_ref, yc_ref, w_ref, g_ref, b_ref, o_ref, *, alpha):
    d_a = ya_ref.shape[1]
    d_b = yb_ref.shape[1]
    h = (_dot(ya_ref[...], w_ref[0:d_a, :]) + _dot(yb_ref[...], w_ref[d_a:d_a + d_b, :])
         + _dot(yc_ref[...], w_ref[d_a + d_b:, :]))
    o_ref[...] = _layer_norm(alpha * x_ref[...] + h, g_ref[...], b_ref[...])


def _out_proj(x, ya, yb, yc, w, g, b, alpha):
    t, d = x.shape
    tm = ROW_TILE
    row = lambda n: pl.BlockSpec((tm, n), lambda i: (i, 0))
    full = lambda a: pl.BlockSpec(a.shape, lambda i: (0, 0))
    return pl.pallas_call(
        functools.partial(_out_proj_kernel, alpha=alpha),
        out_shape=jax.ShapeDtypeStruct((t, d), F32),
        grid=(t // tm,),
        in_specs=[row(d), row(ya.shape[1]), row(yb.shape[1]), row(yc.shape[1]), full(w), full(g), full(b)],
        out_specs=row(d),
        compiler_params=_cparams(("arbitrary",), VMEM_LIMIT_SMALL),
        name="out_proj_ln",
    )(x, ya, yb, yc, w, g, b)


def _swiglu(xb, w1, w3, w2):
    h1 = _dot(xb, w1)
    h3 = _dot(xb, w3)
    gate = h1 * (1.0 / (1.0 + jnp.exp(-h1)))
    return _dot((gate * h3).astype(BF16), w2)


def _ffn_kernel(x_ref, w1_ref, w3_ref, w2_ref, g_ref, b_ref, o_ref, *, alpha):
    x = x_ref[...]
    f = _swiglu(x.astype(BF16), w1_ref[...], w3_ref[...], w2_ref[...])
    o_ref[...] = _layer_norm(alpha * x + f, g_ref[...], b_ref[...])


def _ffn(x, w1, w3, w2, g, b, alpha):
    t, d = x.shape
    tm = ROW_TILE
    full = lambda a: pl.BlockSpec(a.shape, lambda i: (0, 0))
    return pl.pallas_call(
        functools.partial(_ffn_kernel, alpha=alpha),
        out_shape=jax.ShapeDtypeStruct((t, d), F32),
        grid=(t // tm,),
        in_specs=[pl.BlockSpec((tm, d), lambda i: (i, 0)), full(w1), full(w3), full(w2), full(g), full(b)],
        out_specs=pl.BlockSpec((tm, d), lambda i: (i, 0)),
        compiler_params=_cparams(("arbitrary",), VMEM_LIMIT_LARGE),
        name="ffn_ln",
    )(x, w1, w3, w2, g, b)


def _route_tile(x, wh_ref, wl_ref, upper_ref, r_ref, cnt_ref, carry_ref):
    i = pl.program_id(0)
    tr = x.shape[0]
    ne = N_EXPERTS

    @pl.when(i == 0)
    def _():
        carry_ref[...] = jnp.zeros(carry_ref.shape, F32)

    xh, xl = _split_bf16(x)
    logits = _dot(xh, wh_ref[...]) + _dot(xl, wh_ref[...]) + _dot(xh, wl_ref[...])
    lt = jnp.transpose(logits)[0:ne, :]

    eio = lax.broadcasted_iota(jnp.int32, (ne, tr), 0)
    v1 = jnp.max(lt, axis=0, keepdims=True)
    i1 = jnp.min(jnp.where(lt == v1, eio, ne), axis=0, keepdims=True)
    oh1 = eio == i1
    lt2 = jnp.where(oh1, -jnp.inf, lt)
    v2 = jnp.max(lt2, axis=0, keepdims=True)
    i2 = jnp.min(jnp.where(lt2 == v2, eio, ne), axis=0, keepdims=True)
    oh2 = eio == i2
    e = jnp.exp(v2 - v1)
    g1 = 1.0 / (1.0 + e)
    g2 = e / (1.0 + e)

    member = jnp.where(oh1, 1.0, jnp.where(oh2, 1.0, 0.0))
    ranks = _dot(member.astype(BF16), upper_ref[...]) + carry_ref[:, 0:1]
    r1 = jnp.sum(jnp.where(oh1, ranks, 0.0), axis=0, keepdims=True)
    r2 = jnp.sum(jnp.where(oh2, ranks, 0.0), axis=0, keepdims=True)
    carry_ref[...] = carry_ref[...] + jnp.sum(member, axis=1, keepdims=True)
    cnt_ref[...] = carry_ref[...]

    rows = [i1.astype(F32), i2.astype(F32), g1, g2, r1, r2]
    out = jnp.zeros((ne, tr), F32)
    for k, row in enumerate(rows):
        out = jnp.where(eio == k, row, out)
    r_ref[...] = out


def _router_kernel(x_ref, wh_ref, wl_ref, upper_ref, r_ref, cnt_ref, carry_ref):
    _route_tile(x_ref[...], wh_ref, wl_ref, upper_ref, r_ref, cnt_ref, carry_ref)


def _router(x, router_w):
    t, d = x.shape
    tr = ROW_TILE
    rw = jnp.zeros((d, LANES), F32).at[:, :N_EXPERTS].set(router_w)
    wh = rw.astype(BF16)
    wl = (rw - wh.astype(F32)).astype(BF16)
    upper = jnp.triu(jnp.ones((tr, tr), BF16), k=1)
    return pl.pallas_call(
        _router_kernel,
        out_shape=(jax.ShapeDtypeStruct((N_EXPERTS, t), F32), jax.ShapeDtypeStruct((N_EXPERTS, LANES), F32)),
        grid=(t // tr,),
        in_specs=[pl.BlockSpec((tr, d), lambda i: (i, 0)),
                  pl.BlockSpec(wh.shape, lambda i: (0, 0)),
                  pl.BlockSpec(wl.shape, lambda i: (0, 0)),
                  pl.BlockSpec(upper.shape, lambda i: (0, 0))],
        out_specs=(pl.BlockSpec((N_EXPERTS, tr), lambda i: (0, i)),
                   pl.BlockSpec((N_EXPERTS, LANES), lambda i: (0, 0))),
        scratch_shapes=[pltpu.VMEM((N_EXPERTS, LANES), F32)],
        compiler_params=_cparams(("arbitrary",), VMEM_LIMIT_SMALL),
        name="router",
    )(x, wh, wl, upper)


def _row_copies(idx_smem, n_lists, make_copy):
    rows = idx_smem.shape[0]
    per_list = rows * LANES // n_lists
    for r in range(rows):
        for c in range(LANES):
            k, j = divmod(r * LANES + c, per_list)
            make_copy(k, j // SUBLANES, j % SUBLANES, idx_smem[r, c]).start(priority=c % 2)


def _index_stream(pos_hbm, idx_ref, idx_sem):
    return lambda t, sl: pltpu.make_async_copy(pos_hbm.at[t], idx_ref.at[sl], idx_sem.at[sl])


def _dispatch_kernel(zpos_ref, zlen_ref, pos_hbm, x_ref, xs_hbm, xbuf, zbuf, idx_ref, row_sem, idx_sem, zero_sem):
    i = pl.program_id(0)
    n = pl.num_programs(0)
    td = x_ref.shape[0]
    slot = i % 2
    zrows = zbuf.shape[0]

    def zero_region(r, act):
        start, length = zpos_ref[r], zlen_ref[r]
        n_big = length // zrows
        rest = start + n_big * zrows

        def big(q, carry):
            act(pltpu.make_async_copy(
                zbuf, xs_hbm.at[pl.ds(pl.multiple_of(start + q * zrows, SUBLANES), zrows)], zero_sem))
            return carry

        def small(q, carry):
            act(pltpu.make_async_copy(
                zbuf.at[0:SUBLANES], xs_hbm.at[pl.ds(pl.multiple_of(rest + q * SUBLANES, SUBLANES), SUBLANES)],
                zero_sem))
            return carry

        lax.fori_loop(0, n_big, big, 0)
        lax.fori_loop(0, (length - n_big * zrows) // SUBLANES, small, 0)

    @pl.when(i == 0)
    def _():
        zbuf[...] = jnp.zeros(zbuf.shape, F32)
        for r in range(zpos_ref.shape[0]):
            zero_region(r, lambda cp: cp.start())
        for r in range(zpos_ref.shape[0]):
            zero_region(r, lambda cp: cp.wait())

    def wait_slot(sl):
        for _ in range(2):
            pltpu.make_async_copy(xbuf.at[sl], xbuf.at[sl], row_sem.at[sl]).wait()

    @pl.when(i >= 2)
    def _():
        wait_slot(slot)

    idx_copy = _index_stream(pos_hbm, idx_ref, idx_sem)

    @pl.when(i == 0)
    def _():
        idx_copy(0, 0).start()

    for sl in range(2):
        @pl.when(slot == sl)
        def _(sl=sl):
            idx_copy(i, sl).wait()

            @pl.when(i + 1 < n)
            def _():
                idx_copy(i + 1, 1 - sl).start()

            xbuf[sl] = x_ref[...].reshape(xbuf.shape[1:])
            _row_copies(idx_ref.at[sl], 2,
                        lambda k, g, u, p: pltpu.make_async_copy(xbuf.at[sl, g, pl.ds(u, 1)],
                                                                 xs_hbm.at[pl.ds(p, 1)], row_sem.at[sl]))

    @pl.when(i == n - 1)
    def _():
        wait_slot(slot)

        @pl.when(n >= 2)
        def _():
            wait_slot(1 - slot)


def _dispatch(zpos, zlen, pos, x, n_rows):
    t, d = x.shape
    td = COMBINE_TILE
    grid_spec = pltpu.PrefetchScalarGridSpec(
        num_scalar_prefetch=2,
        grid=(t // td,),
        in_specs=[pl.BlockSpec(memory_space=pl.ANY),
                  pl.BlockSpec((td, d), lambda i, zp, zl: (i, 0))],
        out_specs=pl.BlockSpec(memory_space=pl.ANY),
        scratch_shapes=[pltpu.VMEM((2, td // SUBLANES, SUBLANES, d), F32),
                        pltpu.VMEM((td, d), F32),
                        pltpu.SMEM((2, 2 * td // LANES, LANES), jnp.int32),
                        pltpu.SemaphoreType.DMA((2,)),
                        pltpu.SemaphoreType.DMA((2,)),
                        pltpu.SemaphoreType.DMA(())],
    )
    return pl.pallas_call(
        _dispatch_kernel,
        out_shape=jax.ShapeDtypeStruct((n_rows, d), F32),
        grid_spec=grid_spec,
        compiler_params=_cparams(("arbitrary",), VMEM_LIMIT_SMALL),
        name="moe_dispatch",
    )(zpos, zlen, pos, x)


def _moe_kernel(te_ref, na_ref, x_ref, w1_ref, w3_ref, w2_ref, ys_ref):
    active = pl.program_id(0) < na_ref[0]

    @pl.when(active)
    def _():
        ys_ref[...] = _swiglu(x_ref[...].astype(BF16), w1_ref[...], w3_ref[...], w2_ref[...])

    @pl.when(jnp.logical_not(active))
    def _():
        ys_ref[...] = jnp.zeros(ys_ref.shape, F32)


def _moe_experts(tile_expert, n_active, xs, n_tiles, tm, w1, w3, w2):
    d = xs.shape[1]
    f = w1.shape[2]
    grid_spec = pltpu.PrefetchScalarGridSpec(
        num_scalar_prefetch=2,
        grid=(n_tiles,),
        in_specs=[pl.BlockSpec((tm, d), lambda i, te, na: (jnp.minimum(i, na[0] - 1), 0)),
                  pl.BlockSpec((None, d, f), lambda i, te, na: (te[i], 0, 0)),
                  pl.BlockSpec((None, d, f), lambda i, te, na: (te[i], 0, 0)),
                  pl.BlockSpec((None, f, d), lambda i, te, na: (te[i], 0, 0))],
        out_specs=pl.BlockSpec((tm, d), lambda i, te, na: (i, 0)),
    )
    return pl.pallas_call(
        _moe_kernel,
        out_shape=jax.ShapeDtypeStruct((n_tiles * tm, d), F32),
        grid_spec=grid_spec,
        compiler_params=_cparams(("arbitrary",), VMEM_LIMIT_EXPERTS),
        name="moe_experts",
    )(tile_expert, n_active, xs, w1, w3, w2)


def _combine_kernel(pos_hbm, gate_ref, x_ref, ys_hbm, g_ref, b_ref, o_ref,
                    ybuf, idx_ref, row_sem, idx_sem, *, alpha):
    s = pl.program_id(0)
    n = pl.num_programs(0) - 1

    idx_copy = _index_stream(pos_hbm, idx_ref, idx_sem)

    @pl.when(s == 0)
    def _():
        idx_copy(0, 0).start()

    for sl in range(2):
        @pl.when(jnp.logical_and(s < n, s % 2 == sl))
        def _(sl=sl):
            idx_copy(s, sl).wait()

            @pl.when(s + 1 < n)
            def _():
                idx_copy(s + 1, 1 - sl).start()

            _row_copies(idx_ref.at[sl], 2,
                        lambda k, g, u, p: pltpu.make_async_copy(ys_hbm.at[pl.ds(p, 1)],
                                                                 ybuf.at[sl, k, g, pl.ds(u, 1)], row_sem.at[sl]))

    @pl.when(s > 0)
    def _():
        slot = (s - 1) % 2
        pltpu.make_async_copy(ybuf.at[slot], ybuf.at[slot], row_sem.at[slot]).wait()
        gates = gate_ref[...]
        f = (gates[:, 0:1] * ybuf[slot, 0].reshape(x_ref.shape)
             + gates[:, 1:2] * ybuf[slot, 1].reshape(x_ref.shape))
        o_ref[...] = _layer_norm(alpha * x_ref[...] + f, g_ref[...], b_ref[...])


def _combine(pos, gates, x, ys, g, b, alpha):
    t, d = x.shape
    tc = COMBINE_TILE
    n = t // tc
    done = lambda s: (jnp.maximum(s - 1, 0), 0)
    return pl.pallas_call(
        functools.partial(_combine_kernel, alpha=alpha),
        out_shape=jax.ShapeDtypeStruct((t, d), F32),
        grid=(n + 1,),
        in_specs=[pl.BlockSpec(memory_space=pl.ANY),
                  pl.BlockSpec((tc, 2), done),
                  pl.BlockSpec((tc, d), done),
                  pl.BlockSpec(memory_space=pl.ANY),
                  pl.BlockSpec(g.shape, lambda s: (0, 0)),
                  pl.BlockSpec(b.shape, lambda s: (0, 0))],
        out_specs=pl.BlockSpec((tc, d), done),
        scratch_shapes=[pltpu.VMEM((2, 2, tc // SUBLANES, SUBLANES, d), F32),
                        pltpu.SMEM((2, 2 * tc // LANES, LANES), jnp.int32),
                        pltpu.SemaphoreType.DMA((2,)),
                        pltpu.SemaphoreType.DMA((2,))],
        compiler_params=_cparams(("arbitrary",), VMEM_LIMIT_SMALL),
        name="moe_combine_ln",
    )(pos, gates, x, ys, g, b)


def _moe_ffn(x, router_w, w1, w3, w2, g, b, alpha):
    t, d = x.shape
    tm = MOE_TILE
    routed, counts = _router(x, router_w)

    e1 = routed[0].astype(jnp.int32)
    e2 = routed[1].astype(jnp.int32)
    r1 = routed[4].astype(jnp.int32)
    r2 = routed[5].astype(jnp.int32)
    cnt = counts[:, 0].astype(jnp.int32)
    tiles_per_expert = (cnt + tm - 1) // tm
    tile_end = jnp.cumsum(tiles_per_expert)
    row_start = (tile_end - tiles_per_expert) * tm
    pos1 = row_start[e1] + r1
    pos2 = row_start[e2] + r2
    n_tiles = (2 * t) // tm + N_EXPERTS
    n_active = tile_end[-1:]
    tile_ids = jnp.minimum(jnp.arange(n_tiles), n_active[0] - 1)
    tile_expert = jnp.minimum(jnp.sum(tile_ids[:, None] >= tile_end[None, :], axis=1),
                              N_EXPERTS - 1).astype(jnp.int32)
    tc = COMBINE_TILE
    pos = jnp.stack([pos1.reshape(t // tc, tc), pos2.reshape(t // tc, tc)], axis=1).reshape(
        t // tc, 2 * tc // LANES, LANES)
    used = tile_end[-1:] * tm
    zpos = jnp.concatenate([(row_start + cnt) // SUBLANES * SUBLANES, used]).astype(jnp.int32)
    zend = jnp.concatenate([tile_end * tm, jnp.full((1,), n_tiles * tm, jnp.int32)]).astype(jnp.int32)
    xs = _dispatch(zpos, zend - zpos, pos, x, n_tiles * tm)
    ys = _moe_experts(tile_expert, n_active.astype(jnp.int32), xs, n_tiles, tm, w1, w3, w2)

    gates = jnp.stack([routed[2], routed[3]], axis=1)
    return _combine(pos, gates, x, ys, g, b, alpha)


def kernel(x, w_in, w_out, gmlp_ws, gmlp_bs, lam_q1, lam_k1, lam_q2, lam_k2, diff_subln_g, rel_bias,
           pool_w, pool_scale, ln1_g, ln1_b, ln2_g, ln2_b, ffn_w1, ffn_w3, ffn_w2, router_w,
           moe_w1, moe_w3, moe_w2):
    batch, seq_len, d_model = x.shape
    depth = w_in.shape[0]
    d_a = w_out.shape[1] // 4
    d_b = H_B * 2 * DH_B
    d_c = pool_w.shape[1] * pool_w.shape[2]
    alpha = (2 * depth) ** 0.25
    t = batch * seq_len
    o_a = 2 * d_a
    o_q = o_a + d_b
    o_k = o_q + d_b
    o_v = o_k + d_b

    xt = x.reshape(t, d_model)
    bias = _attn_bias_tiles(rel_bias, ATTN_BLOCK)
    q_scale = DH_B ** -0.5 * math.log2(math.e)

    for l in range(depth):
        wl = w_in[l]
        w_cat = jnp.concatenate([wl[:, :o_a], wl[:, o_v:], wl[:, o_a:o_q] * q_scale, wl[:, o_q:o_v]],
                                axis=1).astype(BF16)
        wcat = jnp.transpose(gmlp_ws[l], (1, 0, 2)).reshape(CHUNK, H_A * CHUNK).astype(BF16)
        bmat = jnp.repeat(gmlp_bs[l].T, d_a // H_A, axis=1)
        pw = jax.scipy.linalg.block_diag(*[pool_w[l, gi] for gi in range(G_C)]).astype(BF16)
        ps = pool_scale[l].reshape(1, d_c)
        lam_init = 0.8 - 0.6 * math.exp(-0.3 * l)
        lam = (jnp.exp(jnp.sum(lam_q1[l] * lam_k1[l])) - jnp.exp(jnp.sum(lam_q2[l] * lam_k2[l]))
               + lam_init).reshape(1).astype(F32)
        gain = (diff_subln_g[l] * (1.0 - lam_init)).reshape(2 * DH_B, 1)

        qkv, ya, yc = _in_mix(xt, w_cat, wcat, bmat, pw, ps, seq_len, d_a, d_c)
        yb = _attention(qkv, bias, lam, gain, batch, seq_len)
        g1 = ln1_g[l].reshape(1, -1)
        b1 = ln1_b[l].reshape(1, -1)
        g2 = ln2_g[l].reshape(1, -1)
        b2 = ln2_b[l].reshape(1, -1)
        xt = _out_proj(xt, ya, yb, yc, w_out[l].astype(BF16), g1, b1, alpha)
        i = l // 2
        if l % 2 == 0:
            xt = _ffn(xt, ffn_w1[i].astype(BF16), ffn_w3[i].astype(BF16), ffn_w2[i].astype(BF16), g2, b2, alpha)
        else:
            xt = _moe_ffn(xt, router_w[i], moe_w1[i].astype(BF16), moe_w3[i].astype(BF16),
                          moe_w2[i].astype(BF16), g2, b2, alpha)
    return xt.reshape(batch, seq_len, d_model)
```

```python
import functools
import math

import jax
import jax.numpy as jnp
from jax import lax
from jax.experimental import pallas as pl
from jax.experimental.pallas import tpu as pltpu

F32 = jnp.float32
BF16 = jnp.bfloat16

H_A = 4
CHUNK = 128
H_B = 4
DH_B = 64
G_C = 4
POOL_WINDOWS = (2, 4, 8, 16)
POOL_HALO = 32
NUM_BUCKETS = 32
MAX_EXACT = NUM_BUCKETS // 2
MAX_DISTANCE = 128
N_EXPERTS = 8
EPS = 1e-5

LANES = 128
SUBLANES = 8
ONES_ROWS = 16
MIB = 1024 * 1024
VMEM_BYTES = 64 * MIB
VMEM_LIMIT_SMALL = 3 * VMEM_BYTES // 4
VMEM_LIMIT_LARGE = 7 * VMEM_BYTES // 8
VMEM_LIMIT_EXPERTS = 15 * VMEM_BYTES // 16

ROW_TILE = 512
ATTN_BLOCK = 512
MOE_TILE = 512
COMBINE_TILE = 512


def _cparams(semantics, vmem_limit):
    return pltpu.CompilerParams(dimension_semantics=semantics, vmem_limit_bytes=vmem_limit)


def _dot(a, b):
    return jnp.dot(a, b, preferred_element_type=F32)


def _layer_norm(r, g, b):
    mu = jnp.mean(r, axis=-1, keepdims=True)
    d = r - mu
    var = jnp.mean(d * d, axis=-1, keepdims=True)
    return d * lax.rsqrt(var + EPS) * g + b


def _split_bf16(a):
    hi = a.astype(BF16)
    lo = (a - hi.astype(F32)).astype(BF16)
    return hi, lo


def _in_mix_kernel(x_ref, w_ref, wcat_ref, bmat_ref, pw_ref, ps_ref,
                   qkv_ref, ya_ref, yc_ref, ext_ref, halo_ref, *, seq_len):
    i = pl.program_id(0)
    tm = x_ref.shape[0]
    d_a = ya_ref.shape[1]
    d_c = yc_ref.shape[1]
    hd = d_a // H_A
    gd = d_c // G_C
    n_mix = 2 * d_a + d_c

    xb = x_ref[...].astype(BF16)
    pf = _dot(xb, w_ref[:, :n_mix])
    qkv_ref[...] = _dot(xb, w_ref[:, n_mix:]).astype(BF16)

    z = pf[:, :2 * d_a]
    g = 0.5 * z * (1.0 + lax.erf(z * (1.0 / math.sqrt(2.0))))
    u = g[:, :d_a]
    v = g[:, d_a:]
    ri = lax.broadcasted_iota(jnp.int32, (d_a, d_a), 0) // hd
    ci = lax.broadcasted_iota(jnp.int32, (d_a, d_a), 1) // hd
    grp = jnp.where(ri == ci, 1.0, 0.0).astype(BF16)

    def group_mean(a):
        hi, lo = _split_bf16(a)
        return (_dot(hi, grp) + _dot(lo, grp)) * (1.0 / hd)

    mu = group_mean(v)
    dv = v - mu
    var = group_mean(dv * dv)
    vn = dv * lax.rsqrt(var + EPS)

    wr = lax.broadcasted_iota(jnp.int32, wcat_ref.shape, 0)
    wc = lax.broadcasted_iota(jnp.int32, wcat_ref.shape, 1) % CHUNK
    wcat = jnp.where(wc <= wr, wcat_ref[...], jnp.zeros_like(wcat_ref[...]))
    lane_head = lax.broadcasted_iota(jnp.int32, (CHUNK, d_a), 1) // hd
    bmat = bmat_ref[...]
    for ck in range(tm // CHUNK):
        rows = slice(ck * CHUNK, (ck + 1) * CHUNK)
        vn_c = vn[rows]
        stacked = jnp.concatenate(
            [jnp.where(lane_head == h, vn_c, 0.0).astype(BF16) for h in range(H_A)], axis=0)
        sg = _dot(wcat, stacked) + bmat
        ya_ref[rows, :] = (u[rows] * sg).astype(ya_ref.dtype)

    c = pf[:, 2 * d_a:]
    pos0 = (i * tm) % seq_len

    @pl.when(pos0 == 0)
    def _():
        halo_ref[...] = jnp.zeros(halo_ref.shape, F32)

    ext_ref[0:POOL_HALO, :] = halo_ref[...]
    ext_ref[POOL_HALO:, :] = c
    halo_ref[...] = c[tm - POOL_HALO:, :]
    lane_grp = lax.broadcasted_iota(jnp.int32, (tm, d_c), 1) // gd
    wsum = jnp.zeros_like(c)
    win = jnp.zeros((tm, d_c), jnp.int32)
    for gi, w in enumerate(POOL_WINDOWS):
        assert w == 2 ** (gi + 1)
        lo = SUBLANES * (gi + 1)
        level = ext_ref[lo:, :] + ext_ref[pl.ds(lo - w // 2, POOL_HALO + tm - lo), :]
        wsum = jnp.where(lane_grp == gi, level[POOL_HALO - lo:, :], wsum)
        win = jnp.where(lane_grp == gi, w, win)
        if gi + 1 < len(POOL_WINDOWS):
            ext_ref[lo:, :] = level
    pos = pos0 + lax.broadcasted_iota(jnp.int32, (tm, d_c), 0)
    cnt = jnp.minimum(pos + 1, win).astype(F32)
    y = (wsum / cnt - c).astype(BF16)
    yc_ref[...] = (_dot(y, pw_ref[...]) * ps_ref[...]).astype(yc_ref.dtype)


def _in_mix(x, w, wcat, bmat, pw, ps, seq_len, d_a, d_c):
    t, d = x.shape
    n = w.shape[1]
    n_qkv = n - 2 * d_a - d_c
    tm = ROW_TILE
    full = lambda a: pl.BlockSpec(a.shape, lambda i: (0, 0))
    return pl.pallas_call(
        functools.partial(_in_mix_kernel, seq_len=seq_len),
        out_shape=(jax.ShapeDtypeStruct((t, n_qkv), BF16), jax.ShapeDtypeStruct((t, d_a), BF16),
                   jax.ShapeDtypeStruct((t, d_c), BF16)),
        grid=(t // tm,),
        in_specs=[pl.BlockSpec((tm, d), lambda i: (i, 0)), full(w), full(wcat), full(bmat), full(pw), full(ps)],
        out_specs=(pl.BlockSpec((tm, n_qkv), lambda i: (i, 0)),
                   pl.BlockSpec((tm, d_a), lambda i: (i, 0)),
                   pl.BlockSpec((tm, d_c), lambda i: (i, 0))),
        scratch_shapes=[pltpu.VMEM((POOL_HALO + tm, d_c), F32), pltpu.VMEM((POOL_HALO, d_c), F32)],
        compiler_params=_cparams(("arbitrary",), VMEM_LIMIT_SMALL),
        name="in_proj_mix_ac",
    )(x, w, wcat, bmat, pw, ps)


def _attn_kernel(lam_ref, q_ref, k_ref, v_ref, bias_ref, g_ref, o_ref,
                 qs_ref, m_ref, l_ref, acc_ref, s0_ref, s1_ref, c0_ref, c1_ref):
    i = pl.program_id(2)
    tb, dv = q_ref.shape

    q = q_ref[...]
    lane = lax.broadcasted_iota(jnp.int32, (tb, dv), 1)
    qs_ref[0:tb, :] = jnp.where(lane < DH_B, q, jnp.zeros_like(q))
    qs_ref[tb:, :] = jnp.where(lane >= DH_B, q, jnp.zeros_like(q))
    m_ref[...] = jnp.full(m_ref.shape, -jnp.inf, F32)
    l_ref[...] = jnp.zeros(l_ref.shape, F32)
    acc_ref[...] = jnp.zeros(acc_ref.shape, F32)

    bufs = ((s0_ref, c0_ref), (s1_ref, c1_ref))

    def scores(buf, j, bias=None):
        s_ref, cm_ref = buf
        kj = k_ref[pl.ds(pl.multiple_of(j * tb, tb), tb), :]
        s = lax.dot_general(kj, qs_ref[...], (((1,), (1,)), ((), ())), preferred_element_type=F32)
        if bias is not None:
            s = s + jnp.concatenate([bias, bias], axis=1)
        s_ref[...] = s
        cm_ref[...] = jnp.max(s, axis=0, keepdims=True)

    def consume(buf, j, bias=None):
        s_ref, cm_ref = buf
        vjt = jnp.concatenate([jnp.transpose(v_ref[pl.ds(pl.multiple_of(j * tb, tb), tb), :]),
                               jnp.ones((ONES_ROWS, tb), BF16)], axis=0)
        s = s_ref[...]
        cm = cm_ref[...]
        if bias is not None:
            s = s + jnp.concatenate([bias, bias], axis=1)
            cm = jnp.max(s, axis=0, keepdims=True)
        m_prev = m_ref[...]
        m_next = jnp.maximum(m_prev, cm)
        alpha = jnp.exp2(m_prev - m_next)
        p = jnp.exp2(s - m_next)
        pv = _dot(vjt, p.astype(BF16))
        l_ref[...] = alpha * l_ref[...] + pv[dv:dv + 1]
        m_ref[...] = m_next
        acc_ref[...] = alpha * acc_ref[...] + pv[:dv]

    def pipelined(base, n):
        for u in range(n):
            scores(bufs[(u + 1) % 2], base + u + 1)
            consume(bufs[u % 2], base + u)

    n_far = jnp.maximum(i - 1, 0)

    @pl.when(i == 0)
    def _():
        scores(bufs[0], 0, bias_ref[1])
        consume(bufs[0], 0)

    @pl.when(i == 1)
    def _():
        scores(bufs[0], 0, bias_ref[0])
        scores(bufs[1], 1, bias_ref[1])
        consume(bufs[0], 0)
        consume(bufs[1], 1)

    @pl.when(i >= 2)
    def _():
        scores(bufs[0], 0)

        def oct_body(t, carry):
            pipelined(8 * t, 8)
            return carry

        lax.fori_loop(0, n_far // 8, oct_body, 0)

        def quad_body(t, carry):
            pipelined((n_far // 8) * 8 + 4 * t, 4)
            return carry

        lax.fori_loop(0, (n_far % 8) // 4, quad_body, 0)

        def pair_body(t, carry):
            pipelined((n_far // 4) * 4 + 2 * t, 2)
            return carry

        lax.fori_loop(0, (n_far % 4) // 2, pair_body, 0)

        @pl.when(n_far % 2 == 0)
        def _():
            scores(bufs[1], i, bias_ref[1])
            consume(bufs[0], i - 1, bias_ref[0])
            consume(bufs[1], i)

        @pl.when(n_far % 2 == 1)
        def _():
            scores(bufs[1], i - 1, bias_ref[0])
            consume(bufs[0], i - 2)
            scores(bufs[0], i, bias_ref[1])
            consume(bufs[1], i - 1)
            consume(bufs[0], i)

    o = acc_ref[...] / l_ref[...]
    d = o[:, :tb] - lam_ref[0] * o[:, tb:]
    ms = jnp.mean(d * d, axis=0, keepdims=True)
    y = d * lax.rsqrt(ms + EPS) * g_ref[...]
    o_ref[...] = jnp.transpose(y).astype(o_ref.dtype)


def _attention(qkv, bias, lam, gain, batch, seq_len):
    t = qkv.shape[0]
    tb = ATTN_BLOCK
    nq = seq_len // tb
    dv = 2 * DH_B
    return pl.pallas_call(
        _attn_kernel,
        out_shape=jax.ShapeDtypeStruct((t, H_B * dv), BF16),
        grid=(batch, H_B, nq),
        in_specs=[pl.BlockSpec(memory_space=pltpu.SMEM),
                  pl.BlockSpec((tb, dv), lambda b, h, i: (b * nq + i, h)),
                  pl.BlockSpec((seq_len, dv), lambda b, h, i: (b, H_B + h)),
                  pl.BlockSpec((seq_len, dv), lambda b, h, i: (b, 2 * H_B + h)),
                  pl.BlockSpec((None, 2, tb, tb), lambda b, h, i: (h, 0, 0, 0)),
                  pl.BlockSpec((dv, 1), lambda b, h, i: (0, 0))],
        out_specs=pl.BlockSpec((tb, dv), lambda b, h, i: (b * nq + i, h)),
        scratch_shapes=[pltpu.VMEM((2 * tb, dv), BF16),
                        pltpu.VMEM((1, 2 * tb), F32),
                        pltpu.VMEM((1, 2 * tb), F32),
                        pltpu.VMEM((dv, 2 * tb), F32),
                        pltpu.VMEM((tb, 2 * tb), F32),
                        pltpu.VMEM((tb, 2 * tb), F32),
                        pltpu.VMEM((1, 2 * tb), F32),
                        pltpu.VMEM((1, 2 * tb), F32)],
        compiler_params=_cparams(("arbitrary", "arbitrary", "arbitrary"), VMEM_LIMIT_LARGE),
        name="diff_attn",
    )(lam, qkv, qkv, qkv, bias, gain)


def _bias_kernel(rel_ref, o_ref):
    h = pl.program_id(0)
    tb = o_ref.shape[-1]
    kk = lax.broadcasted_iota(jnp.int32, (tb, tb), 0)
    qq = lax.broadcasted_iota(jnp.int32, (tb, tb), 1)
    far = rel_ref[(NUM_BUCKETS - 1) * H_B + h]
    for w, off in enumerate((tb, 0)):
        dist = off + qq - kk
        n = jnp.maximum(dist, 0)
        nf = jnp.maximum(n, 1).astype(F32)
        large = MAX_EXACT + (jnp.log(nf / MAX_EXACT) / math.log(MAX_DISTANCE / MAX_EXACT)
                             * (NUM_BUCKETS - MAX_EXACT)).astype(jnp.int32)
        bucket = jnp.where(n < MAX_EXACT, n, jnp.minimum(large, NUM_BUCKETS - 1))
        b = jnp.zeros((tb, tb), F32)
        for k in range(NUM_BUCKETS):
            b = jnp.where(bucket == k, rel_ref[k * H_B + h], b)
        o_ref[w] = jnp.where(dist >= 0, (b - far) * math.log2(math.e), -jnp.inf)


def _attn_bias_tiles(rel_bias, tb):
    return pl.pallas_call(
        _bias_kernel,
        out_shape=jax.ShapeDtypeStruct((H_B, 2, tb, tb), F32),
        grid=(H_B,),
        in_specs=[pl.BlockSpec(memory_space=pltpu.SMEM)],
        out_specs=pl.BlockSpec((None, 2, tb, tb), lambda h: (h, 0, 0, 0)),
        compiler_params=_cparams(("arbitrary",), VMEM_LIMIT_SMALL),
        name="attn_bias",
    )(rel_bias.reshape(NUM_BUCKETS * H_B))


def _out_proj_kernel(x_ref, ya_ref, yb_ref, yc_ref, w_ref, g_ref, b_ref, o_ref, *, alpha):
    d_a = ya_ref.shape[1]
    d_b = yb_ref.shape[1]
    h = (_dot(ya_ref[...], w_ref[0:d_a, :]) + _dot(yb_ref[...], w_ref[d_a:d_a + d_b, :])
         + _dot(yc_ref[...], w_ref[d_a + d_b:, :]))
    o_ref[...] = _layer_norm(alpha * x_ref[...] + h, g_ref[...], b_ref[...])


def _out_proj(x, ya, yb, yc, w, g, b, alpha):
    t, d = x.shape
    tm = ROW_TILE
    row = lambda n: pl.BlockSpec((tm, n), lambda i: (i, 0))
    full = lambda a: pl.BlockSpec(a.shape, lambda i: (0, 0))
    return pl.pallas_call(
        functools.partial(_out_proj_kernel, alpha=alpha),
        out_shape=jax.ShapeDtypeStruct((t, d), F32),
        grid=(t // tm,),
        in_specs=[row(d), row(ya.shape[1]), row(yb.shape[1]), row(yc.shape[1]), full(w), full(g), full(b)],
        out_specs=row(d),
        compiler_params=_cparams(("arbitrary",), VMEM_LIMIT_SMALL),
        name="out_proj_ln",
    )(x, ya, yb, yc, w, g, b)


def _swiglu(xb, w1, w3, w2):
    h1 = _dot(xb, w1)
    h3 = _dot(xb, w3)
    gate = h1 * (1.0 / (1.0 + jnp.exp(-h1)))
    return _dot((gate * h3).astype(BF16), w2)


def _ffn_kernel(x_ref, w1_ref, w3_ref, w2_ref, g_ref, b_ref, o_ref, *, alpha):
    x = x_ref[...]
    f = _swiglu(x.astype(BF16), w1_ref[...], w3_ref[...], w2_ref[...])
    o_ref[...] = _layer_norm(alpha * x + f, g_ref[...], b_ref[...])


def _ffn(x, w1, w3, w2, g, b, alpha):
    t, d = x.shape
    tm = ROW_TILE
    full = lambda a: pl.BlockSpec(a.shape, lambda i: (0, 0))
    return pl.pallas_call(
        functools.partial(_ffn_kernel, alpha=alpha),
        out_shape=jax.ShapeDtypeStruct((t, d), F32),
        grid=(t // tm,),
        in_specs=[pl.BlockSpec((tm, d), lambda i: (i, 0)), full(w1), full(w3), full(w2), full(g), full(b)],
        out_specs=pl.BlockSpec((tm, d), lambda i: (i, 0)),
        compiler_params=_cparams(("arbitrary",), VMEM_LIMIT_LARGE),
        name="ffn_ln",
    )(x, w1, w3, w2, g, b)


def _route_tile(x, wh_ref, wl_ref, upper_ref, r_ref, cnt_ref, carry_ref):
    i = pl.program_id(0)
    tr = x.shape[0]
    ne = N_EXPERTS

    @pl.when(i == 0)
    def _():
        carry_ref[...] = jnp.zeros(carry_ref.shape, F32)

    xh, xl = _split_bf16(x)
    logits = _dot(xh, wh_ref[...]) + _dot(xl, wh_ref[...]) + _dot(xh, wl_ref[...])
    lt = jnp.transpose(logits)[0:ne, :]

    eio = lax.broadcasted_iota(jnp.int32, (ne, tr), 0)
    v1 = jnp.max(lt, axis=0, keepdims=True)
    i1 = jnp.min(jnp.where(lt == v1, eio, ne), axis=0, keepdims=True)
    oh1 = eio == i1
    lt2 = jnp.where(oh1, -jnp.inf, lt)
    v2 = jnp.max(lt2, axis=0, keepdims=True)
    i2 = jnp.min(jnp.where(lt2 == v2, eio, ne), axis=0, keepdims=True)
    oh2 = eio == i2
    e = jnp.exp(v2 - v1)
    g1 = 1.0 / (1.0 + e)
    g2 = e / (1.0 + e)

    member = jnp.where(oh1, 1.0, jnp.where(oh2, 1.0, 0.0))
    ranks = _dot(member.astype(BF16), upper_ref[...]) + carry_ref[:, 0:1]
    r1 = jnp.sum(jnp.where(oh1, ranks, 0.0), axis=0, keepdims=True)
    r2 = jnp.sum(jnp.where(oh2, ranks, 0.0), axis=0, keepdims=True)
    carry_ref[...] = carry_ref[...] + jnp.sum(member, axis=1, keepdims=True)
    cnt_ref[...] = carry_ref[...]

    rows = [i1.astype(F32), i2.astype(F32), g1, g2, r1, r2]
    out = jnp.zeros((ne, tr), F32)
    for k, row in enumerate(rows):
        out = jnp.where(eio == k, row, out)
    r_ref[...] = out


def _router_kernel(x_ref, wh_ref, wl_ref, upper_ref, r_ref, cnt_ref, carry_ref):
    _route_tile(x_ref[...], wh_ref, wl_ref, upper_ref, r_ref, cnt_ref, carry_ref)


def _router(x, router_w):
    t, d = x.shape
    tr = ROW_TILE
    rw = jnp.zeros((d, LANES), F32).at[:, :N_EXPERTS].set(router_w)
    wh = rw.astype(BF16)
    wl = (rw - wh.astype(F32)).astype(BF16)
    upper = jnp.triu(jnp.ones((tr, tr), BF16), k=1)
    return pl.pallas_call(
        _router_kernel,
        out_shape=(jax.ShapeDtypeStruct((N_EXPERTS, t), F32), jax.ShapeDtypeStruct((N_EXPERTS, LANES), F32)),
        grid=(t // tr,),
        in_specs=[pl.BlockSpec((tr, d), lambda i: (i, 0)),
                  pl.BlockSpec(wh.shape, lambda i: (0, 0)),
                  pl.BlockSpec(wl.shape, lambda i: (0, 0)),
                  pl.BlockSpec(upper.shape, lambda i: (0, 0))],
        out_specs=(pl.BlockSpec((N_EXPERTS, tr), lambda i: (0, i)),
                   pl.BlockSpec((N_EXPERTS, LANES), lambda i: (0, 0))),
        scratch_shapes=[pltpu.VMEM((N_EXPERTS, LANES), F32)],
        compiler_params=_cparams(("arbitrary",), VMEM_LIMIT_SMALL),
        name="router",
    )(x, wh, wl, upper)


def _row_copies(idx_smem, n_lists, make_copy):
    rows = idx_smem.shape[0]
    per_list = rows * LANES // n_lists
    for r in range(rows):
        for c in range(LANES):
            k, j = divmod(r * LANES + c, per_list)
            make_copy(k, j // SUBLANES, j % SUBLANES, idx_smem[r, c]).start(priority=c % 2)


def _index_stream(pos_hbm, idx_ref, idx_sem):
    return lambda t, sl: pltpu.make_async_copy(pos_hbm.at[t], idx_ref.at[sl], idx_sem.at[sl])


def _dispatch_kernel(zpos_ref, zlen_ref, pos_hbm, x_ref, xs_hbm, xbuf, zbuf, idx_ref, row_sem, idx_sem, zero_sem):
    i = pl.program_id(0)
    n = pl.num_programs(0)
    td = x_ref.shape[0]
    slot = i % 2
    zrows = zbuf.shape[0]

    def zero_region(r, act):
        start, length = zpos_ref[r], zlen_ref[r]
        n_big = length // zrows
        rest = start + n_big * zrows

        def big(q, carry):
            act(pltpu.make_async_copy(
                zbuf, xs_hbm.at[pl.ds(pl.multiple_of(start + q * zrows, SUBLANES), zrows)], zero_sem))
            return carry

        def small(q, carry):
            act(pltpu.make_async_copy(
                zbuf.at[0:SUBLANES], xs_hbm.at[pl.ds(pl.multiple_of(rest + q * SUBLANES, SUBLANES), SUBLANES)],
                zero_sem))
            return carry

        lax.fori_loop(0, n_big, big, 0)
        lax.fori_loop(0, (length - n_big * zrows) // SUBLANES, small, 0)

    @pl.when(i == 0)
    def _():
        zbuf[...] = jnp.zeros(zbuf.shape, F32)
        for r in range(zpos_ref.shape[0]):
            zero_region(r, lambda cp: cp.start())
        for r in range(zpos_ref.shape[0]):
            zero_region(r, lambda cp: cp.wait())

    def wait_slot(sl):
        for _ in range(2):
            pltpu.make_async_copy(xbuf.at[sl], xbuf.at[sl], row_sem.at[sl]).wait()

    @pl.when(i >= 2)
    def _():
        wait_slot(slot)

    idx_copy = _index_stream(pos_hbm, idx_ref, idx_sem)

    @pl.when(i == 0)
    def _():
        idx_copy(0, 0).start()

    for sl in range(2):
        @pl.when(slot == sl)
        def _(sl=sl):
            idx_copy(i, sl).wait()

            @pl.when(i + 1 < n)
            def _():
                idx_copy(i + 1, 1 - sl).start()

            xbuf[sl] = x_ref[...].reshape(xbuf.shape[1:])
            _row_copies(idx_ref.at[sl], 2,
                        lambda k, g, u, p: pltpu.make_async_copy(xbuf.at[sl, g, pl.ds(u, 1)],
                                                                 xs_hbm.at[pl.ds(p, 1)], row_sem.at[sl]))

    @pl.when(i == n - 1)
    def _():
        wait_slot(slot)

        @pl.when(n >= 2)
        def _():
            wait_slot(1 - slot)


def _dispatch(zpos, zlen, pos, x, n_rows):
    t, d = x.shape
    td = COMBINE_TILE
    grid_spec = pltpu.PrefetchScalarGridSpec(
        num_scalar_prefetch=2,
        grid=(t // td,),
        in_specs=[pl.BlockSpec(memory_space=pl.ANY),
                  pl.BlockSpec((td, d), lambda i, zp, zl: (i, 0))],
        out_specs=pl.BlockSpec(memory_space=pl.ANY),
        scratch_shapes=[pltpu.VMEM((2, td // SUBLANES, SUBLANES, d), F32),
                        pltpu.VMEM((td, d), F32),
                        pltpu.SMEM((2, 2 * td // LANES, LANES), jnp.int32),
                        pltpu.SemaphoreType.DMA((2,)),
                        pltpu.SemaphoreType.DMA((2,)),
                        pltpu.SemaphoreType.DMA(())],
    )
    return pl.pallas_call(
        _dispatch_kernel,
        out_shape=jax.ShapeDtypeStruct((n_rows, d), F32),
        grid_spec=grid_spec,
        compiler_params=_cparams(("arbitrary",), VMEM_LIMIT_SMALL),
        name="moe_dispatch",
    )(zpos, zlen, pos, x)


def _moe_kernel(te_ref, na_ref, x_ref, w1_ref, w3_ref, w2_ref, ys_ref):
    active = pl.program_id(0) < na_ref[0]

    @pl.when(active)
    def _():
        ys_ref[...] = _swiglu(x_ref[...].astype(BF16), w1_ref[...], w3_ref[...], w2_ref[...])

    @pl.when(jnp.logical_not(active))
    def _():
        ys_ref[...] = jnp.zeros(ys_ref.shape, F32)


def _moe_experts(tile_expert, n_active, xs, n_tiles, tm, w1, w3, w2):
    d = xs.shape[1]
    f = w1.shape[2]
    grid_spec = pltpu.PrefetchScalarGridSpec(
        num_scalar_prefetch=2,
        grid=(n_tiles,),
        in_specs=[pl.BlockSpec((tm, d), lambda i, te, na: (jnp.minimum(i, na[0] - 1), 0)),
                  pl.BlockSpec((None, d, f), lambda i, te, na: (te[i], 0, 0)),
                  pl.BlockSpec((None, d, f), lambda i, te, na: (te[i], 0, 0)),
                  pl.BlockSpec((None, f, d), lambda i, te, na: (te[i], 0, 0))],
        out_specs=pl.BlockSpec((tm, d), lambda i, te, na: (i, 0)),
    )
    return pl.pallas_call(
        _moe_kernel,
        out_shape=jax.ShapeDtypeStruct((n_tiles * tm, d), F32),
        grid_spec=grid_spec,
        compiler_params=_cparams(("arbitrary",), VMEM_LIMIT_EXPERTS),
        name="moe_experts",
    )(tile_expert, n_active, xs, w1, w3, w2)


def _combine_kernel(pos_hbm, gate_ref, x_ref, ys_hbm, g_ref, b_ref, o_ref,
                    ybuf, idx_ref, row_sem, idx_sem, *, alpha):
    s = pl.program_id(0)
    n = pl.num_programs(0) - 1

    idx_copy = _index_stream(pos_hbm, idx_ref, idx_sem)

    @pl.when(s == 0)
    def _():
        idx_copy(0, 0).start()

    for sl in range(2):
        @pl.when(jnp.logical_and(s < n, s % 2 == sl))
        def _(sl=sl):
            idx_copy(s, sl).wait()

            @pl.when(s + 1 < n)
            def _():
                idx_copy(s + 1, 1 - sl).start()

            _row_copies(idx_ref.at[sl], 2,
                        lambda k, g, u, p: pltpu.make_async_copy(ys_hbm.at[pl.ds(p, 1)],
                                                                 ybuf.at[sl, k, g, pl.ds(u, 1)], row_sem.at[sl]))

    @pl.when(s > 0)
    def _():
        slot = (s - 1) % 2
        pltpu.make_async_copy(ybuf.at[slot], ybuf.at[slot], row_sem.at[slot]).wait()
        gates = gate_ref[...]
        f = (gates[:, 0:1] * ybuf[slot, 0].reshape(x_ref.shape)
             + gates[:, 1:2] * ybuf[slot, 1].reshape(x_ref.shape))
        o_ref[...] = _layer_norm(alpha * x_ref[...] + f, g_ref[...], b_ref[...])


def _combine(pos, gates, x, ys, g, b, alpha):
    t, d = x.shape
    tc = COMBINE_TILE
    n = t // tc
    done = lambda s: (jnp.maximum(s - 1, 0), 0)
    return pl.pallas_call(
        functools.partial(_combine_kernel, alpha=alpha),
        out_shape=jax.ShapeDtypeStruct((t, d), F32),
        grid=(n + 1,),
        in_specs=[pl.BlockSpec(memory_space=pl.ANY),
                  pl.BlockSpec((tc, 2), done),
                  pl.BlockSpec((tc, d), done),
                  pl.BlockSpec(memory_space=pl.ANY),
                  pl.BlockSpec(g.shape, lambda s: (0, 0)),
                  pl.BlockSpec(b.shape, lambda s: (0, 0))],
        out_specs=pl.BlockSpec((tc, d), done),
        scratch_shapes=[pltpu.VMEM((2, 2, tc // SUBLANES, SUBLANES, d), F32),
                        pltpu.SMEM((2, 2 * tc // LANES, LANES), jnp.int32),
                        pltpu.SemaphoreType.DMA((2,)),
                        pltpu.SemaphoreType.DMA((2,))],
        compiler_params=_cparams(("arbitrary",), VMEM_LIMIT_SMALL),
        name="moe_combine_ln",
    )(pos, gates, x, ys, g, b)


def _moe_ffn(x, router_w, w1, w3, w2, g, b, alpha):
    t, d = x.shape
    tm = MOE_TILE
    routed, counts = _router(x, router_w)

    e1 = routed[0].astype(jnp.int32)
    e2 = routed[1].astype(jnp.int32)
    r1 = routed[4].astype(jnp.int32)
    r2 = routed[5].astype(jnp.int32)
    cnt = counts[:, 0].astype(jnp.int32)
    tiles_per_expert = (cnt + tm - 1) // tm
    tile_end = jnp.cumsum(tiles_per_expert)
    row_start = (tile_end - tiles_per_expert) * tm
    pos1 = row_start[e1] + r1
    pos2 = row_start[e2] + r2
    n_tiles = (2 * t) // tm + N_EXPERTS
    n_active = tile_end[-1:]
    tile_ids = jnp.minimum(jnp.arange(n_tiles), n_active[0] - 1)
    tile_expert = jnp.minimum(jnp.sum(tile_ids[:, None] >= tile_end[None, :], axis=1),
                              N_EXPERTS - 1).astype(jnp.int32)
    tc = COMBINE_TILE
    pos = jnp.stack([pos1.reshape(t // tc, tc), pos2.reshape(t // tc, tc)], axis=1).reshape(
        t // tc, 2 * tc // LANES, LANES)
    used = tile_end[-1:] * tm
    zpos = jnp.concatenate([(row_start + cnt) // SUBLANES * SUBLANES, used]).astype(jnp.int32)
    zend = jnp.concatenate([tile_end * tm, jnp.full((1,), n_tiles * tm, jnp.int32)]).astype(jnp.int32)
    xs = _dispatch(zpos, zend - zpos, pos, x, n_tiles * tm)
    ys = _moe_experts(tile_expert, n_active.astype(jnp.int32), xs, n_tiles, tm, w1, w3, w2)

    gates = jnp.stack([routed[2], routed[3]], axis=1)
    return _combine(pos, gates, x, ys, g, b, alpha)


def kernel(x, w_in, w_out, gmlp_ws, gmlp_bs, lam_q1, lam_k1, lam_q2, lam_k2, diff_subln_g, rel_bias,
           pool_w, pool_scale, ln1_g, ln1_b, ln2_g, ln2_b, ffn_w1, ffn_w3, ffn_w2, router_w,
           moe_w1, moe_w3, moe_w2):
    batch, seq_len, d_model = x.shape
    depth = w_in.shape[0]
    d_a = w_out.shape[1] // 4
    d_b = H_B * 2 * DH_B
    d_c = pool_w.shape[1] * pool_w.shape[2]
    alpha = (2 * depth) ** 0.25
    t = batch * seq_len
    o_a = 2 * d_a
    o_q = o_a + d_b
    o_k = o_q + d_b
    o_v = o_k + d_b

    xt = x.reshape(t, d_model)
    bias = _attn_bias_tiles(rel_bias, ATTN_BLOCK)
    q_scale = DH_B ** -0.5 * math.log2(math.e)

    for l in range(depth):
        wl = w_in[l]
        w_cat = jnp.concatenate([wl[:, :o_a], wl[:, o_v:], wl[:, o_a:o_q] * q_scale, wl[:, o_q:o_v]],
                                axis=1).astype(BF16)
        wcat = jnp.transpose(gmlp_ws[l], (1, 0, 2)).reshape(CHUNK, H_A * CHUNK).astype(BF16)
        bmat = jnp.repeat(gmlp_bs[l].T, d_a // H_A, axis=1)
        pw = jax.scipy.linalg.block_diag(*[pool_w[l, gi] for gi in range(G_C)]).astype(BF16)
        ps = pool_scale[l].reshape(1, d_c)
        lam_init = 0.8 - 0.6 * math.exp(-0.3 * l)
        lam = (jnp.exp(jnp.sum(lam_q1[l] * lam_k1[l])) - jnp.exp(jnp.sum(lam_q2[l] * lam_k2[l]))
               + lam_init).reshape(1).astype(F32)
        gain = (diff_subln_g[l] * (1.0 - lam_init)).reshape(2 * DH_B, 1)

        qkv, ya, yc = _in_mix(xt, w_cat, wcat, bmat, pw, ps, seq_len, d_a, d_c)
        yb = _attention(qkv, bias, lam, gain, batch, seq_len)
        g1 = ln1_g[l].reshape(1, -1)
        b1 = ln1_b[l].reshape(1, -1)
        g2 = ln2_g[l].reshape(1, -1)
        b2 = ln2_b[l].reshape(1, -1)
        xt = _out_proj(xt, ya, yb, yc, w_out[l].astype(BF16), g1, b1, alpha)
        i = l // 2
        if l % 2 == 0:
            xt = _ffn(xt, ffn_w1[i].astype(BF16), ffn_w3[i].astype(BF16), ffn_w2[i].astype(BF16), g2, b2, alpha)
        else:
            xt = _moe_ffn(xt, router_w[i], moe_w1[i].astype(BF16), moe_w3[i].astype(BF16),
                          moe_w2[i].astype(BF16), g2, b2, alpha)
    return xt.reshape(batch, seq_len, d_model)
```
